```python
import math
import jax, jax.numpy as jnp
from jax import lax
import numpy as np

D_MODEL = 1024
BATCH = 4
SEQ = 8192
DEPTH = 2

SSM_D_INNER = 2 * D_MODEL
SSM_HEAD_DIM = 64
SSM_HEADS = SSM_D_INNER // SSM_HEAD_DIM
SSM_GROUPS = 8
SSM_HEADS_PER_GROUP = SSM_HEADS // SSM_GROUPS
SSM_D_STATE = 128
SSM_CONV = 4
SSM_CHUNK = 128
SSM_CONV_DIM = SSM_D_INNER + 2 * SSM_GROUPS * SSM_D_STATE
SSM_IN_DIM = SSM_D_INNER + SSM_CONV_DIM + SSM_HEADS

SB_HEADS = 16
SB_HEAD_DIM = D_MODEL // SB_HEADS
SB_BLOCK = 128

FFN_DENSE = 2752
N_EXPERTS = 8
TOP_K = 2
FFN_EXPERT = 3584
MOE_BLOCK = 256

N_EVEN = (DEPTH + 1) // 2
N_ODD = DEPTH // 2
DEEPNORM_ALPHA = (2.0 * DEPTH) ** 0.25
DEEPNORM_BETA = (8.0 * DEPTH) ** -0.25
LN_EPS = 1e-5
RMS_EPS = 1e-5

kernel_name = "hybrid_ssd_stickbreak_moe_deepnorm"


def layer_norm(x, g, b):
    xf = x.astype(jnp.float32)
    mu = jnp.mean(xf, axis=-1, keepdims=True)
    var = jnp.mean(jnp.square(xf - mu), axis=-1, keepdims=True)
    y = (xf - mu) * lax.rsqrt(var + LN_EPS)
    return (y * g.astype(jnp.float32) + b.astype(jnp.float32)).astype(x.dtype)


def causal_depthwise_conv(u, w, b):
    c = u.shape[-1]
    out = lax.conv_general_dilated(
        u, w[:, None, :].astype(u.dtype), window_strides=(1,),
        padding=[(SSM_CONV - 1, 0)], dimension_numbers=("NWC", "WIO", "NWC"),
        feature_group_count=c)
    return out + b.astype(u.dtype)


def ssd_chunked_scan(xs, dt, a, bm, cm):
    bsz, s = xs.shape[0], xs.shape[1]
    n_chunks = s // SSM_CHUNK

    def to_chunks(t):
        return jnp.moveaxis(t.reshape((bsz, n_chunks, SSM_CHUNK) + t.shape[2:]), 1, 0)

    causal = jnp.tril(jnp.ones((SSM_CHUNK, SSM_CHUNK), dtype=bool))[None, :, :, None, None]

    def step(state, inp):
        xc, dtc, bc, cc = inp
        acum = jnp.cumsum(dtc * a, axis=1)
        seg = acum[:, :, None] - acum[:, None, :]
        decay = jnp.exp(jnp.where(causal, seg, -jnp.inf))
        cb = jnp.einsum("btgn,bsgn->btsg", cc, bc)
        w = cb[..., None] * decay * dtc[:, None]
        y_diag = jnp.einsum("btsgh,bsghp->btghp", w, xc)
        y_off = jnp.einsum("btgn,bghpn->btghp", cc, state) * jnp.exp(acum)[..., None]
        decay_end = jnp.exp(acum[:, -1:] - acum) * dtc
        new_state = (state * jnp.exp(acum[:, -1])[..., None, None]
                     + jnp.einsum("bsgn,bsgh,bsghp->bghpn", bc, decay_end, xc))
        return new_state, y_diag + y_off

    state0 = jnp.zeros((bsz, SSM_GROUPS, SSM_HEADS_PER_GROUP, SSM_HEAD_DIM, SSM_D_STATE), jnp.float32)
    _, ys = lax.scan(step, state0, (to_chunks(xs), to_chunks(dt), to_chunks(bm), to_chunks(cm)))
    return jnp.moveaxis(ys, 0, 1).reshape(xs.shape)


def mamba2_mixer(x, w_in, conv_w, conv_b, dt_bias, a_log, d_skip, norm_w, w_out):
    bsz, s, _ = x.shape
    zxbcdt = x @ w_in
    z, xbc, dt_raw = jnp.split(zxbcdt, [SSM_D_INNER, SSM_D_INNER + SSM_CONV_DIM], axis=-1)
    xbc = jax.nn.silu(causal_depthwise_conv(xbc, conv_w, conv_b))
    xs, bm, cm = jnp.split(xbc, [SSM_D_INNER, SSM_D_INNER + SSM_GROUPS * SSM_D_STATE], axis=-1)
    xs = xs.astype(jnp.float32).reshape(bsz, s, SSM_GROUPS, SSM_HEADS_PER_GROUP, SSM_HEAD_DIM)
    bm = bm.astype(jnp.float32).reshape(bsz, s, SSM_GROUPS, SSM_D_STATE)
    cm = cm.astype(jnp.float32).reshape(bsz, s, SSM_GROUPS, SSM_D_STATE)
    dt = jax.nn.softplus(dt_raw.astype(jnp.float32) + dt_bias.astype(jnp.float32))
    dt = dt.reshape(bsz, s, SSM_GROUPS, SSM_HEADS_PER_GROUP)
    a = -jnp.exp(a_log.astype(jnp.float32)).reshape(SSM_GROUPS, SSM_HEADS_PER_GROUP)
    y = ssd_chunked_scan(xs, dt, a, bm, cm)
    y = y + d_skip.astype(jnp.float32).reshape(SSM_GROUPS, SSM_HEADS_PER_GROUP)[:, :, None] * xs
    y = y.reshape(bsz, s, SSM_GROUPS, -1) * jax.nn.silu(z.astype(jnp.float32)).reshape(bsz, s, SSM_GROUPS, -1)
    y = y * lax.rsqrt(jnp.mean(jnp.square(y), axis=-1, keepdims=True) + RMS_EPS)
    y = y.reshape(bsz, s, SSM_D_INNER) * norm_w.astype(jnp.float32)
    return y.astype(x.dtype) @ w_out


def stick_breaking_attention(x, w_qkv, w_o):
    bsz, s, _ = x.shape
    qkv = (x @ w_qkv).astype(jnp.float32).reshape(bsz, s, 3, SB_HEADS, SB_HEAD_DIM)
    qkv = qkv.transpose(2, 0, 3, 1, 4)
    q, k, v = qkv[0], qkv[1], qkv[2]
    n_blocks = s // SB_BLOCK
    scale = 1.0 / math.sqrt(SB_HEAD_DIM)
    offs = jnp.arange(SB_BLOCK)

    def query_block(i):
        q_blk = lax.dynamic_slice_in_dim(q, i * SB_BLOCK, SB_BLOCK, axis=2)
        t_pos = i * SB_BLOCK + offs

        def body(m, carry):
            acc_log, out = carry
            j = i - m
            k_blk = lax.dynamic_slice_in_dim(k, j * SB_BLOCK, SB_BLOCK, axis=2)
            v_blk = lax.dynamic_slice_in_dim(v, j * SB_BLOCK, SB_BLOCK, axis=2)
            z = jnp.einsum("bhtd,bhsd->bhts", q_blk, k_blk) * scale
            s_pos = j * SB_BLOCK + offs
            mask = s_pos[None, :] < t_pos[:, None]
            log_not = jnp.where(mask, jax.nn.log_sigmoid(-z), 0.0)
            suffix_excl = lax.cumsum(log_not, axis=3, reverse=True) - log_not
            log_a = jax.nn.log_sigmoid(z) + suffix_excl + acc_log[..., None]
            attn = jnp.where(mask, jnp.exp(log_a), 0.0)
            out = out + jnp.einsum("bhts,bhsd->bhtd", attn, v_blk)
            return acc_log + jnp.sum(log_not, axis=-1), out

        init = (jnp.zeros((bsz, SB_HEADS, SB_BLOCK), jnp.float32),
                jnp.zeros((bsz, SB_HEADS, SB_BLOCK, SB_HEAD_DIM), jnp.float32))
        _, out = lax.fori_loop(0, i + 1, body, init)
        return out

    outs = lax.map(query_block, jnp.arange(n_blocks))
    outs = outs.transpose(1, 0, 3, 2, 4).reshape(bsz, s, SB_HEADS * SB_HEAD_DIM)
    return outs.astype(x.dtype) @ w_o


def swiglu(x, w_gate, w_up, w_down):
    return (jax.nn.silu(x @ w_gate) * (x @ w_up)) @ w_down


def moe_swiglu(x, w_router, b_router, w_gate, w_up, w_down):
    bsz, s, d = x.shape
    n_tok = bsz * s
    xf = x.reshape(n_tok, d)
    logits = (xf @ w_router).astype(jnp.float32) + b_router.astype(jnp.float32)
    top_logits, top_idx = lax.top_k(logits, TOP_K)
    gates = jax.nn.softmax(top_logits, axis=-1)
    n_assign = n_tok * TOP_K
    expert_flat = top_idx.reshape(-1)
    order = jnp.argsort(expert_flat)
    sorted_expert = expert_flat[order]
    counts = jnp.bincount(expert_flat, length=N_EXPERTS)
    padded = (counts + MOE_BLOCK - 1) // MOE_BLOCK * MOE_BLOCK
    pad_end = jnp.cumsum(padded)
    pad_start = pad_end - padded
    start = jnp.cumsum(counts) - counts
    dest = pad_start[sorted_expert] + (jnp.arange(n_assign) - start[sorted_expert])
    n_slots = n_assign + N_EXPERTS * MOE_BLOCK
    n_blocks = n_slots // MOE_BLOCK
    token = order // TOP_K
    x_buf = jnp.zeros((n_slots, d), x.dtype).at[dest].set(xf[token])
    block_start = jnp.arange(n_blocks) * MOE_BLOCK
    block_expert = jnp.minimum(jnp.sum(pad_end[None, :] <= block_start[:, None], axis=1), N_EXPERTS - 1)

    def expert_block(args):
        xb, e = args
        return swiglu(xb, w_gate[e], w_up[e], w_down[e])

    y_buf = lax.map(expert_block, (x_buf.reshape(n_blocks, MOE_BLOCK, d), block_expert))
    y_buf = y_buf.reshape(n_slots, d)
    gate_sorted = gates.reshape(-1)[order]
    y = jnp.zeros((n_tok, d), jnp.float32).at[token].add(y_buf[dest].astype(jnp.float32) * gate_sorted[:, None])
    return y.reshape(bsz, s, d).astype(x.dtype)


def setup_inputs(seed: int = 0) -> dict:
    key = jax.random.key(seed)
    ks = jax.random.split(key, 24)
    f32 = jnp.float32
    nrm = lambda k, shape, sc: jax.random.normal(k, shape, f32) * sc
    dt0 = jnp.exp(jax.random.uniform(ks[5], (N_EVEN, SSM_HEADS), f32,
                                     math.log(1e-3), math.log(1e-1)))
    v_scale = jnp.concatenate([jnp.ones((2 * D_MODEL,), f32), jnp.full((D_MODEL,), DEEPNORM_BETA, f32)])
    return {
        "x": nrm(ks[0], (BATCH, SEQ, D_MODEL), 1.0),
        "ln_g": 1.0 + nrm(ks[1], (DEPTH, 2, D_MODEL), 0.02),
        "ln_b": nrm(ks[2], (DEPTH, 2, D_MODEL), 0.02),
        "ssm_w_in": nrm(ks[3], (N_EVEN, D_MODEL, SSM_IN_DIM), D_MODEL ** -0.5),
        "ssm_conv_w": nrm(ks[4], (N_EVEN, SSM_CONV, SSM_CONV_DIM), SSM_CONV ** -0.5),
        "ssm_conv_b": nrm(ks[6], (N_EVEN, SSM_CONV_DIM), 0.02),
        "ssm_dt_bias": dt0 + jnp.log(-jnp.expm1(-dt0)),
        "ssm_a_log": jnp.log(jax.random.uniform(ks[7], (N_EVEN, SSM_HEADS), f32, 1.0, 16.0)),
        "ssm_d": 1.0 + nrm(ks[8], (N_EVEN, SSM_HEADS), 0.1),
        "ssm_norm_w": 1.0 + nrm(ks[9], (N_EVEN, SSM_D_INNER), 0.02),
        "ssm_w_out": nrm(ks[10], (N_EVEN, SSM_D_INNER, D_MODEL), SSM_D_INNER ** -0.5 * DEEPNORM_BETA),
        "sb_w_qkv": nrm(ks[11], (N_ODD, D_MODEL, 3 * D_MODEL), D_MODEL ** -0.5) * v_scale,
        "sb_w_o": nrm(ks[12], (N_ODD, D_MODEL, D_MODEL), D_MODEL ** -0.5 * DEEPNORM_BETA),
        "ffn_w_gate": nrm(ks[13], (N_EVEN, D_MODEL, FFN_DENSE), D_MODEL ** -0.5),
        "ffn_w_up": nrm(ks[14], (N_EVEN, D_MODEL, FFN_DENSE), D_MODEL ** -0.5),
        "ffn_w_down": nrm(ks[15], (N_EVEN, FFN_DENSE, D_MODEL), FFN_DENSE ** -0.5 * DEEPNORM_BETA),
        "moe_w_router": nrm(ks[16], (N_ODD, D_MODEL, N_EXPERTS), D_MODEL ** -0.5),
        "moe_b_router": nrm(ks[17], (N_ODD, N_EXPERTS), 0.01),
        "moe_w_gate": nrm(ks[18], (N_ODD, N_EXPERTS, D_MODEL, FFN_EXPERT), D_MODEL ** -0.5),
        "moe_w_up": nrm(ks[19], (N_ODD, N_EXPERTS, D_MODEL, FFN_EXPERT), D_MODEL ** -0.5),
        "moe_w_down": nrm(ks[20], (N_ODD, N_EXPERTS, FFN_EXPERT, D_MODEL), FFN_EXPERT ** -0.5 * DEEPNORM_BETA),
    }


def reference(x, ln_g, ln_b, ssm_w_in, ssm_conv_w, ssm_conv_b, ssm_dt_bias, ssm_a_log, ssm_d,
              ssm_norm_w, ssm_w_out, sb_w_qkv, sb_w_o, ffn_w_gate, ffn_w_up, ffn_w_down,
              moe_w_router, moe_b_router, moe_w_gate, moe_w_up, moe_w_down):
    for layer in range(DEPTH):
        p = layer // 2
        if layer % 2 == 0:
            h = mamba2_mixer(x, ssm_w_in[p], ssm_conv_w[p], ssm_conv_b[p], ssm_dt_bias[p],
                             ssm_a_log[p], ssm_d[p], ssm_norm_w[p], ssm_w_out[p])
        else:
            h = stick_breaking_attention(x, sb_w_qkv[p], sb_w_o[p])
        x = layer_norm(DEEPNORM_ALPHA * x + h, ln_g[layer, 0], ln_b[layer, 0])
        if layer % 2 == 0:
            f = swiglu(x, ffn_w_gate[p], ffn_w_up[p], ffn_w_down[p])
        else:
            f = moe_swiglu(x, moe_w_router[p], moe_b_router[p], moe_w_gate[p], moe_w_up[p], moe_w_down[p])
        x = layer_norm(DEEPNORM_ALPHA * x + f, ln_g[layer, 1], ln_b[layer, 1])
    return x
```

```python
import functools
import math

import jax
import jax.numpy as jnp
from jax import lax
from jax.experimental import pallas as pl
from jax.experimental.pallas import tpu as pltpu

F32 = jnp.float32
BF16 = jnp.bfloat16

LANES_V7X = 128
SUBLANES_V7X = 8
VMEM_LIMIT_BYTES_V7X = 56 * 1024 * 1024

DEPTH = 2
SSM_HEAD_DIM = 64
SSM_GROUPS = 8
SSM_D_STATE = 128
SSM_CONV = 4
SSM_CHUNK = 128
SB_HEAD_DIM = 64
N_EXPERTS = 8
DEEPNORM_ALPHA = (2.0 * DEPTH) ** 0.25
LN_EPS = 1e-5
RMS_EPS = 1e-5
LOG2E = 1.4426950408889634
LN2 = 0.6931471805599453

ROW_TILE = 512
MOE_ROWS = 512
ATTN_Q_TILE = 512
ATTN_K_TILE = 128
ROUTE_TILE = 1024


def _params(*semantics):
    return pltpu.CompilerParams(dimension_semantics=semantics,
                                vmem_limit_bytes=VMEM_LIMIT_BYTES_V7X)


def _layer_norm_rows(y, g, b):
    mu = jnp.mean(y, axis=-1, keepdims=True)
    d = y - mu
    var = jnp.mean(d * d, axis=-1, keepdims=True)
    return d * lax.rsqrt(var + LN_EPS) * g + b


def _silu(x):
    return x * (1.0 / (1.0 + jnp.exp(-x)))


def _softplus(x):
    return jnp.maximum(x, 0.0) + jnp.log(1.0 + jnp.exp(-jnp.abs(x)))


def _bdot(a, b):
    return jnp.dot(a.astype(BF16), b.astype(BF16), preferred_element_type=F32)


def _mm_kernel(x_ref, w_ref, o_ref):
    o_ref[...] = jnp.dot(x_ref[...], w_ref[...],
                         preferred_element_type=F32).astype(o_ref.dtype)


def _matmul(x, w, out_dtype, tm, tn, name):
    m, k = x.shape
    n = w.shape[1]
    return pl.pallas_call(
        _mm_kernel,
        grid=(m // tm, n // tn),
        in_specs=[pl.BlockSpec((tm, k), lambda i, j: (i, 0)),
                  pl.BlockSpec((k, tn), lambda i, j: (0, j))],
        out_specs=pl.BlockSpec((tm, tn), lambda i, j: (i, j)),
        out_shape=jax.ShapeDtypeStruct((m, n), out_dtype),
        compiler_params=_params("parallel", "arbitrary"),
        name=name,
    )(x, w)


def _mm_ln_kernel(a_ref, w_ref, res_ref, g_ref, b_ref, o_ref, ob_ref):
    h = jnp.dot(a_ref[...], w_ref[...], preferred_element_type=F32)
    o = _layer_norm_rows(DEEPNORM_ALPHA * res_ref[...] + h, g_ref[...], b_ref[...])
    o_ref[...] = o
    ob_ref[...] = o.astype(BF16)


def _matmul_ln(a, w, res, g, b, tm, name):
    m, k = a.shape
    n = w.shape[1]
    return pl.pallas_call(
        _mm_ln_kernel,
        grid=(m // tm,),
        in_specs=[pl.BlockSpec((tm, k), lambda i: (i, 0)),
                  pl.BlockSpec((k, n), lambda i: (0, 0)),
                  pl.BlockSpec((tm, n), lambda i: (i, 0)),
                  pl.BlockSpec((1, n), lambda i: (0, 0)),
                  pl.BlockSpec((1, n), lambda i: (0, 0))],
        out_specs=[pl.BlockSpec((tm, n), lambda i: (i, 0)),
                   pl.BlockSpec((tm, n), lambda i: (i, 0))],
        out_shape=[jax.ShapeDtypeStruct((m, n), F32),
                   jax.ShapeDtypeStruct((m, n), BF16)],
        compiler_params=_params("parallel"),
        name=name,
    )(a, w, res, g, b)


def _ffn_body(eid_ref, nused_ref, x_ref, wg_ref, wu_ref, wd_ref, acc_ref):
    j = pl.program_id(1)
    x = x_ref[...].astype(BF16)
    g = jnp.dot(x, wg_ref[0], preferred_element_type=F32)
    u = jnp.dot(x, wu_ref[0], preferred_element_type=F32)
    h = (_silu(g) * u).astype(BF16)
    part = jnp.dot(h, wd_ref[0], preferred_element_type=F32)

    @pl.when(j == 0)
    def _():
        acc_ref[...] = part

    @pl.when(j > 0)
    def _():
        acc_ref[...] += part


def _ffn_ln_kernel(eid_ref, nused_ref, x_ref, wg_ref, wu_ref, wd_ref, res_ref, g_ref, b_ref,
                   o_ref, ob_ref, acc_ref):
    _ffn_body(eid_ref, nused_ref, x_ref, wg_ref, wu_ref, wd_ref, acc_ref)

    @pl.when(pl.program_id(1) == pl.num_programs(1) - 1)
    def _():
        o = _layer_norm_rows(DEEPNORM_ALPHA * res_ref[...] + acc_ref[...],
                             g_ref[...], b_ref[...])
        o_ref[...] = o
        ob_ref[...] = o.astype(BF16)


def _ffn_raw_kernel(eid_ref, nused_ref, x_ref, wg_ref, wu_ref, wd_ref, o_ref, acc_ref):
    i = pl.program_id(0)
    last = pl.program_id(1) == pl.num_programs(1) - 1
    used = i < nused_ref[0]

    @pl.when(used)
    def _():
        _ffn_body(eid_ref, nused_ref, x_ref, wg_ref, wu_ref, wd_ref, acc_ref)

    @pl.when(jnp.logical_and(used, last))
    def _():
        o_ref[...] = acc_ref[...]

    @pl.when(jnp.logical_and(jnp.logical_not(used), last))
    def _():
        o_ref[...] = jnp.zeros_like(o_ref)


def _ffn_specs(tm, d, tf, nf):
    def wcol(i, j, eid, nused):
        return (eid[i], 0, jnp.where(i < nused[0], j, nf - 1))

    def wrow(i, j, eid, nused):
        return (eid[i], jnp.where(i < nused[0], j, nf - 1), 0)

    return [pl.BlockSpec((tm, d), lambda i, j, eid, nused: (i, 0)),
            pl.BlockSpec((1, d, tf), wcol),
            pl.BlockSpec((1, d, tf), wcol),
            pl.BlockSpec((1, tf, d), wrow)]


def _ffn_ln(xb, wg, wu, wd, res, g, b, tm, tf, name):
    m, d = xb.shape
    f = wg.shape[2]
    nf = f // tf
    nblk = m // tm
    eid = jnp.zeros((nblk,), jnp.int32)
    nused = jnp.full((1,), nblk, jnp.int32)
    row = lambda i, j, eid, nused: (i, 0)
    vec = lambda i, j, eid, nused: (0, 0)
    return pl.pallas_call(
        _ffn_ln_kernel,
        grid_spec=pltpu.PrefetchScalarGridSpec(
            num_scalar_prefetch=2,
            grid=(nblk, nf),
            in_specs=_ffn_specs(tm, d, tf, nf) + [
                pl.BlockSpec((tm, d), row),
                pl.BlockSpec((1, d), vec),
                pl.BlockSpec((1, d), vec)],
            out_specs=[pl.BlockSpec((tm, d), row), pl.BlockSpec((tm, d), row)],
            scratch_shapes=[pltpu.VMEM((tm, d), F32)]),
        out_shape=[jax.ShapeDtypeStruct((m, d), F32),
                   jax.ShapeDtypeStruct((m, d), BF16)],
        compiler_params=_params("parallel", "arbitrary"),
        name=name,
    )(eid, nused, xb, wg, wu, wd, res, g, b)


def _ffn_experts(x_buf, eid, nused, wg, wu, wd, tm, tf, name):
    m, d = x_buf.shape
    f = wg.shape[2]
    nf = f // tf
    nblk = m // tm
    row = lambda i, j, eid, nused: (i, 0)
    return pl.pallas_call(
        _ffn_raw_kernel,
        grid_spec=pltpu.PrefetchScalarGridSpec(
            num_scalar_prefetch=2,
            grid=(nblk, nf),
            in_specs=_ffn_specs(tm, d, tf, nf),
            out_specs=pl.BlockSpec((tm, d), row),
            scratch_shapes=[pltpu.VMEM((tm, d), F32)]),
        out_shape=jax.ShapeDtypeStruct((m, d), F32),
        compiler_params=_params("parallel", "arbitrary"),
        name=name,
    )(eid, nused, x_buf, wg, wu, wd)


def _split3(v):
    hi = v.astype(BF16)
    r = v - hi.astype(F32)
    mid = r.astype(BF16)
    lo = (r - mid.astype(F32)).astype(BF16)
    return hi, mid, lo


def _expand_cols(v, e):
    hi, mid, lo = _split3(v)
    out = jnp.dot(hi, e, preferred_element_type=F32)
    out += jnp.dot(mid, e, preferred_element_type=F32)
    out += jnp.dot(lo, e, preferred_element_type=F32)
    return out


def _ssd_kernel(xbc_ref, z_ref, dt_ref, convw_ref, convb_ref, dtb_ref, alog_ref, dexp_ref,
                normw_ref, ltri_ref, sel128_ref, sel64_ref, y_ref,
                ext_ref, act_ref, eoff_ref, state_ref, *, d_inner, n_heads):
    L = SSM_CHUNK
    N = SSM_D_STATE
    G = SSM_GROUPS
    hpg = n_heads // G
    gw = d_inner // G
    halo = SUBLANES_V7X

    @pl.when(pl.program_id(1) == 0)
    def _():
        ext_ref[0:halo, :] = jnp.zeros((halo, ext_ref.shape[1]), F32)
        state_ref[...] = jnp.zeros_like(state_ref)

    ext_ref[halo:halo + L, :] = xbc_ref[...]
    conv = convb_ref[...]
    for k in range(SSM_CONV):
        conv = conv + convw_ref[k:k + 1, :] * ext_ref[pl.ds(halo - (SSM_CONV - 1) + k, L), :]
    ext_ref[0:halo, :] = xbc_ref[L - halo:L, :]
    act_ref[...] = _silu(conv)

    dt = _softplus(dt_ref[...] + dtb_ref[...])
    a = -jnp.exp(alog_ref[...])
    acum = jnp.dot(ltri_ref[...], dt * a, preferred_element_type=F32,
                   precision=lax.Precision.HIGHEST)
    acum_last = acum[L - 1:L, :]
    acum_t = acum.T
    dt_t = dt.T
    eoff_ref[...] = _expand_cols(jnp.exp(acum), sel64_ref[...])
    eend = _expand_cols(jnp.exp(acum_last - acum) * dt, sel64_ref[...])

    row = lax.broadcasted_iota(jnp.int32, (L, L), 0)
    col = lax.broadcasted_iota(jnp.int32, (L, L), 1)
    causal = col <= row
    glane = lax.broadcasted_iota(jnp.int32, (L, gw), 1) // SSM_HEAD_DIM

    for g in range(G):
        xs_g = act_ref[:, g * gw:(g + 1) * gw]
        b_g = act_ref[:, d_inner + g * N:d_inner + (g + 1) * N]
        c_g = act_ref[:, d_inner + G * N + g * N:d_inner + G * N + (g + 1) * N]
        b_gt = b_g.T.astype(BF16)
        c_gb = c_g.astype(BF16)
        cb = jnp.dot(c_gb, b_gt, preferred_element_type=F32)
        acol = _expand_cols(acum, sel128_ref[:, g * hpg * L:(g + 1) * hpg * L])
        y_g = jnp.zeros((L, gw), F32)
        for h in range(hpg):
            hh = g * hpg + h
            seg = acol[:, h * L:(h + 1) * L] - acum_t[hh:hh + 1, :]
            decay = jnp.exp(jnp.where(causal, seg, -jnp.inf))
            w = cb * decay * dt_t[hh:hh + 1, :]
            xs_h = jnp.where(glane == h, xs_g, 0.0)
            y_g = y_g + _bdot(w, xs_h)
        state_g = state_ref[:, g * gw:(g + 1) * gw]
        eoff_g = eoff_ref[:, g * gw:(g + 1) * gw]
        y_g = y_g + _bdot(c_gb, state_g) * eoff_g
        xs_scaled = xs_g * eend[:, g * gw:(g + 1) * gw]
        state_ref[:, g * gw:(g + 1) * gw] = (
            state_g * eoff_g[L - 1:L, :]
            + jnp.dot(b_gt, xs_scaled.astype(BF16), preferred_element_type=F32))
        y_g = y_g + dexp_ref[:, g * gw:(g + 1) * gw] * xs_g
        y_g = y_g * _silu(z_ref[:, g * gw:(g + 1) * gw])
        ms = jnp.mean(y_g * y_g, axis=-1, keepdims=True)
        y_g = y_g * lax.rsqrt(ms + RMS_EPS) * normw_ref[:, g * gw:(g + 1) * gw]
        y_ref[:, g * gw:(g + 1) * gw] = y_g.astype(y_ref.dtype)


def _ssd(zx, conv_w, conv_b, dt_bias, a_log, d_skip, norm_w, batch, seq, d_inner, n_heads):
    L = SSM_CHUNK
    conv_dim = d_inner + 2 * SSM_GROUPS * SSM_D_STATE
    nc = seq // L
    pad = LANES_V7X - n_heads
    dtb = jnp.pad(dt_bias, (0, pad)).reshape(1, LANES_V7X)
    alog = jnp.pad(a_log, (0, pad)).reshape(1, LANES_V7X)
    dexp = jnp.repeat(d_skip, SSM_HEAD_DIM).reshape(1, d_inner)
    ltri = jnp.tril(jnp.ones((L, L), F32))
    hidx = jnp.arange(LANES_V7X)[:, None]
    sel128 = (hidx == (jnp.arange(n_heads * L) // L)[None, :]).astype(BF16)
    sel64 = (hidx == (jnp.arange(d_inner) // SSM_HEAD_DIM)[None, :]).astype(BF16)
    xbc_blocks = conv_dim // conv_dim
    del xbc_blocks
    z_blk = conv_dim // d_inner
    dt_blk = (conv_dim + d_inner) // LANES_V7X
    const = lambda b, c: (0, 0)
    kern = functools.partial(_ssd_kernel, d_inner=d_inner, n_heads=n_heads)
    return pl.pallas_call(
        kern,
        grid=(batch, nc),
        in_specs=[pl.BlockSpec((L, conv_dim), lambda b, c: (b * nc + c, 0)),
                  pl.BlockSpec((L, d_inner), lambda b, c: (b * nc + c, z_blk)),
                  pl.BlockSpec((L, LANES_V7X), lambda b, c: (b * nc + c, dt_blk)),
                  pl.BlockSpec((SSM_CONV, conv_dim), const),
                  pl.BlockSpec((1, conv_dim), const),
                  pl.BlockSpec((1, LANES_V7X), const),
                  pl.BlockSpec((1, LANES_V7X), const),
                  pl.BlockSpec((1, d_inner), const),
                  pl.BlockSpec((1, d_inner), const),
                  pl.BlockSpec((L, L), const),
                  pl.BlockSpec((LANES_V7X, n_heads * L), const),
                  pl.BlockSpec((LANES_V7X, d_inner), const)],
        out_specs=pl.BlockSpec((L, d_inner), lambda b, c: (b * nc + c, 0)),
        out_shape=jax.ShapeDtypeStruct((batch * seq, d_inner), BF16),
        scratch_shapes=[pltpu.VMEM((L + SUBLANES_V7X, conv_dim), F32),
                        pltpu.VMEM((L, conv_dim), F32),
                        pltpu.VMEM((L, d_inner), F32),
                        pltpu.VMEM((SSM_D_STATE, d_inner), F32)],
        compiler_params=_params("parallel", "arbitrary"),
        name="ssd_scan",
    )(zx, zx, zx, conv_w, conv_b.reshape(1, conv_dim), dtb, alog, dexp,
      norm_w.reshape(1, d_inner), ltri, sel128, sel64)


def _attn_kernel(q_ref, k_ref, v_ref, mbd_ref, o_ref, kk_ref, vv_ref, acc_ref, *, tq):
    tk = ATTN_K_TILE
    qi = pl.program_id(2)
    n_kb = k_ref.shape[0] // tk
    n_diag = tq // tk
    lane = lax.broadcasted_iota(jnp.int32, (tk, LANES_V7X), 1)

    @pl.when(qi == 0)
    def _():
        def fill(j, c):
            kb = k_ref[pl.ds(pl.multiple_of(j * tk, tk), tk), :]
            vb = v_ref[pl.ds(pl.multiple_of(j * tk, tk), tk), :]
            zero = jnp.zeros_like(kb)
            kk_ref[j, 0:tk, :] = jnp.where(lane < SB_HEAD_DIM, kb, zero)
            kk_ref[j, tk:2 * tk, :] = jnp.where(lane >= SB_HEAD_DIM, kb, zero)
            vv_ref[j, 0:tk, :] = jnp.where(lane < SB_HEAD_DIM, vb, zero)
            vv_ref[j, tk:2 * tk, :] = jnp.where(lane >= SB_HEAD_DIM, vb, zero)
            return c
        lax.fori_loop(0, n_kb, fill, 0)

    q = q_ref[...]
    acc_ref[...] = jnp.zeros_like(acc_ref)
    t_pos = qi * tq + lax.broadcasted_iota(jnp.int32, (tq, 2 * tk), 0)
    s_off = lax.broadcasted_iota(jnp.int32, (tq, 2 * tk), 1) % tk

    def step(j, carry, masked):
        a0, a1 = carry
        z = lax.dot_general(q, kk_ref[j], (((1,), (1,)), ((), ())),
                            preferred_element_type=F32)
        sp = jnp.maximum(z, 0.0) + jnp.log2(1.0 + jnp.exp2(-jnp.abs(z)))
        if masked:
            mask = (j * tk + s_off) < t_pos
            sp = jnp.where(mask, sp, 0.0)
        incl = jnp.dot(sp.astype(BF16), mbd_ref[...], preferred_element_type=F32)
        la0 = z[:, :tk] - incl[:, :tk] - a0
        la1 = z[:, tk:] - incl[:, tk:] - a1
        p = jnp.concatenate([jnp.exp2(la0), jnp.exp2(la1)], axis=1)
        if masked:
            p = jnp.where(mask, p, 0.0)
        acc_ref[...] += jnp.dot(p.astype(BF16), vv_ref[j], preferred_element_type=F32)
        a0 = a0 + jnp.sum(sp[:, :tk], axis=-1, keepdims=True)
        a1 = a1 + jnp.sum(sp[:, tk:], axis=-1, keepdims=True)
        return a0, a1

    carry = (jnp.zeros((tq, 1), F32), jnp.zeros((tq, 1), F32))
    j_diag = qi * n_diag
    for d in range(n_diag - 1, -1, -1):
        carry = step(j_diag + d, carry, True)
    lax.fori_loop(0, j_diag, lambda m, c: step(j_diag - 1 - m, c, False), carry)
    o_ref[...] = acc_ref[...].astype(o_ref.dtype)


def _attention(qkv, batch, seq, d_model):
    tq = min(ATTN_Q_TILE, seq)
    tk = ATTN_K_TILE
    nq = seq // tq
    n_pairs = d_model // LANES_V7X
    r = jnp.arange(2 * tk)
    mbd = ((r[:, None] // tk == r[None, :] // tk) & (r[:, None] >= r[None, :])).astype(BF16)
    kern = functools.partial(_attn_kernel, tq=tq)
    return pl.pallas_call(
        kern,
        grid=(batch, n_pairs, nq),
        in_specs=[pl.BlockSpec((tq, LANES_V7X), lambda b, p, i: (b * nq + i, p)),
                  pl.BlockSpec((seq, LANES_V7X), lambda b, p, i: (b, n_pairs + p)),
                  pl.BlockSpec((seq, LANES_V7X), lambda b, p, i: (b, 2 * n_pairs + p)),
                  pl.BlockSpec((2 * tk, 2 * tk), lambda b, p, i: (0, 0))],
        out_specs=pl.BlockSpec((tq, LANES_V7X), lambda b, p, i: (b * nq + i, p)),
        out_shape=jax.ShapeDtypeStruct((batch * seq, d_model), BF16),
        scratch_shapes=[pltpu.VMEM((seq // tk, 2 * tk, LANES_V7X), BF16),
                        pltpu.VMEM((seq // tk, 2 * tk, LANES_V7X), BF16),
                        pltpu.VMEM((tq, LANES_V7X), F32)],
        compiler_params=_params("parallel", "parallel", "arbitrary"),
        name="stickbreak_attn",
    )(qkv, qkv, qkv, mbd)


def _router_kernel(x_ref, w_ref, b_ref, ltri_ref, info_ref, cnt_ref, carry_ref):
    tm = x_ref.shape[0]

    @pl.when(pl.program_id(0) == 0)
    def _():
        carry_ref[...] = jnp.zeros_like(carry_ref)

    logits = jnp.dot(x_ref[...], w_ref[...], preferred_element_type=F32,
                     precision=lax.Precision.HIGHEST) + b_ref[...]
    lane = lax.broadcasted_iota(jnp.int32, (tm, LANES_V7X), 1).astype(F32)
    big = float(LANES_V7X)
    m1 = jnp.max(logits, axis=-1, keepdims=True)
    i1 = jnp.min(jnp.where(logits == m1, lane, big), axis=-1, keepdims=True)
    l2 = jnp.where(lane == i1, -jnp.inf, logits)
    m2 = jnp.max(l2, axis=-1, keepdims=True)
    i2 = jnp.min(jnp.where(l2 == m2, lane, big), axis=-1, keepdims=True)
    e = jnp.exp(m2 - m1)
    g1 = 1.0 / (1.0 + e)
    g2 = e * g1
    hit1 = lane == i1
    hit2 = lane == i2
    oh = jnp.where(jnp.logical_or(hit1, hit2), 1.0, 0.0)
    cum = jnp.dot(ltri_ref[...], oh.astype(BF16), preferred_element_type=F32)
    before = cum - oh + carry_ref[0:1, :]
    r1 = jnp.sum(jnp.where(hit1, before, 0.0), axis=-1, keepdims=True)
    r2 = jnp.sum(jnp.where(hit2, before, 0.0), axis=-1, keepdims=True)
    total = carry_ref[0:1, :] + cum[tm - 1:tm, :]
    carry_ref[...] = jnp.broadcast_to(total, carry_ref.shape)
    cnt_ref[...] = jnp.broadcast_to(total, cnt_ref.shape)
    info = jnp.where(lane == 0.0, i1, 0.0)
    info = jnp.where(lane == 1.0, i2, info)
    info = jnp.where(lane == 2.0, g1, info)
    info = jnp.where(lane == 3.0, g2, info)
    info = jnp.where(lane == 4.0, r1, info)
    info = jnp.where(lane == 5.0, r2, info)
    info_ref[...] = info


def _router(x, w_router, b_router):
    t, d = x.shape
    tm = min(ROUTE_TILE, t)
    pad = LANES_V7X - N_EXPERTS
    w = jnp.pad(w_router, ((0, 0), (0, pad)))
    b = jnp.concatenate([b_router.astype(F32), jnp.full((pad,), -1e30, F32)]).reshape(1, LANES_V7X)
    ltri = jnp.tril(jnp.ones((tm, tm), BF16))
    return pl.pallas_call(
        _router_kernel,
        grid=(t // tm,),
        in_specs=[pl.BlockSpec((tm, d), lambda i: (i, 0)),
                  pl.BlockSpec((d, LANES_V7X), lambda i: (0, 0)),
                  pl.BlockSpec((1, LANES_V7X), lambda i: (0, 0)),
                  pl.BlockSpec((tm, tm), lambda i: (0, 0))],
        out_specs=[pl.BlockSpec((tm, LANES_V7X), lambda i: (i, 0)),
                   pl.BlockSpec((SUBLANES_V7X, LANES_V7X), lambda i: (0, 0))],
        out_shape=[jax.ShapeDtypeStruct((t, LANES_V7X), F32),
                   jax.ShapeDtypeStruct((SUBLANES_V7X, LANES_V7X), F32)],
        scratch_shapes=[pltpu.VMEM((SUBLANES_V7X, LANES_V7X), F32)],
        compiler_params=_params("arbitrary"),
        name="moe_router",
    )(x, w, b, ltri)


def _row_copy(src_ref, src_row, dst_ref, dst_row, sem):
    return pltpu.make_async_copy(src_ref.at[pl.ds(src_row, 1)],
                                 dst_ref.at[pl.ds(dst_row, 1)], sem)


def _dispatch_kernel(d1_ref, d2_ref, x_hbm, buf_in_hbm, buf_hbm, sem):
    del buf_in_hbm
    tm = d1_ref.shape[0]
    base = pl.program_id(0) * tm

    def issue(t, c):
        _row_copy(x_hbm, base + t, buf_hbm, d1_ref[t], sem).start()
        _row_copy(x_hbm, base + t, buf_hbm, d2_ref[t], sem).start()
        return c
    lax.fori_loop(0, tm, issue, 0)

    def drain(t, c):
        _row_copy(x_hbm, 0, buf_hbm, 0, sem).wait()
        _row_copy(x_hbm, 0, buf_hbm, 0, sem).wait()
        return c
    lax.fori_loop(0, tm, drain, 0)


def _dispatch(x, dest1, dest2, n_slots):
    t, d = x.shape
    tm = min(ROUTE_TILE, t)
    buf0 = jnp.zeros((n_slots, d), x.dtype)
    smem = lambda: pl.BlockSpec((tm,), lambda i: (i,), memory_space=pltpu.SMEM)
    return pl.pallas_call(
        _dispatch_kernel,
        grid=(t // tm,),
        in_specs=[smem(), smem(),
                  pl.BlockSpec(memory_space=pl.ANY),
                  pl.BlockSpec(memory_space=pl.ANY)],
        out_specs=pl.BlockSpec(memory_space=pl.ANY),
        out_shape=jax.ShapeDtypeStruct((n_slots, d), x.dtype),
        scratch_shapes=[pltpu.SemaphoreType.DMA],
        input_output_aliases={3: 0},
        compiler_params=_params("arbitrary"),
        name="moe_dispatch",
    )(dest1, dest2, x, buf0)


def _combine_kernel(d1_ref, d2_ref, info_ref, y_hbm, res_ref, g_ref, b_ref, o_ref,
                    buf1_ref, buf2_ref, sem):
    tm = d1_ref.shape[0]

    def issue(t, c):
        _row_copy(y_hbm, d1_ref[t], buf1_ref, t, sem).start()
        _row_copy(y_hbm, d2_ref[t], buf2_ref, t, sem).start()
        return c
    lax.fori_loop(0, tm, issue, 0)

    def drain(t, c):
        _row_copy(y_hbm, 0, buf1_ref, 0, sem).wait()
        _row_copy(y_hbm, 0, buf2_ref, 0, sem).wait()
        return c
    lax.fori_loop(0, tm, drain, 0)

    g1 = info_ref[:, 2:3]
    g2 = info_ref[:, 3:4]
    y = g1 * buf1_ref[...] + g2 * buf2_ref[...]
    o_ref[...] = _layer_norm_rows(DEEPNORM_ALPHA * res_ref[...] + y, g_ref[...], b_ref[...])


def _combine_ln(y_buf, dest1, dest2, info, res, g, b):
    t, d = res.shape
    tm = min(ROUTE_TILE, t)
    smem = lambda: pl.BlockSpec((tm,), lambda i: (i,), memory_space=pltpu.SMEM)
    return pl.pallas_call(
        _combine_kernel,
        grid=(t // tm,),
        in_specs=[smem(), smem(),
                  pl.BlockSpec((tm, LANES_V7X), lambda i: (i, 0)),
                  pl.BlockSpec(memory_space=pl.ANY),
                  pl.BlockSpec((tm, d), lambda i: (i, 0)),
                  pl.BlockSpec((1, d), lambda i: (0, 0)),
                  pl.BlockSpec((1, d), lambda i: (0, 0))],
        out_specs=pl.BlockSpec((tm, d), lambda i: (i, 0)),
        out_shape=jax.ShapeDtypeStruct((t, d), F32),
        scratch_shapes=[pltpu.VMEM((tm, d), F32), pltpu.VMEM((tm, d), F32),
                        pltpu.SemaphoreType.DMA],
        compiler_params=_params("arbitrary"),
        name="moe_combine_ln",
    )(dest1, dest2, info, y_buf, res, g, b)


def _pad_cols(w, n):
    return jnp.pad(w, ((0, 0), (0, n - w.shape[1])))


def _mamba_layer(x, xb, batch, seq, ln_g, ln_b, w_in, conv_w, conv_b, dt_bias, a_log, d_skip,
                 norm_w, w_out, w_gate, w_up, w_down):
    d_model = x.shape[1]
    n_heads = a_log.shape[0]
    d_inner = n_heads * SSM_HEAD_DIM
    conv_dim = conv_w.shape[1]
    w_z, w_xbc, w_dt = jnp.split(w_in, [d_inner, d_inner + conv_dim], axis=1)
    w_in_r = jnp.concatenate([w_xbc, w_z, _pad_cols(w_dt, LANES_V7X)], axis=1).astype(BF16)
    n_in = w_in_r.shape[1]
    tn = n_in // 7 if n_in % (7 * LANES_V7X) == 0 else LANES_V7X
    zx = _matmul(xb, w_in_r, F32, min(ROW_TILE * 2, x.shape[0]), tn, "ssm_in_proj")
    y = _ssd(zx, conv_w, conv_b, dt_bias, a_log, d_skip, norm_w, batch, seq, d_inner, n_heads)
    g0 = ln_g[0].reshape(1, d_model)
    b0 = ln_b[0].reshape(1, d_model)
    x1, x1b = _matmul_ln(y, w_out.astype(BF16), x, g0, b0, min(ROW_TILE, x.shape[0]),
                         "ssm_out_proj_ln")
    f = w_gate.shape[1]
    fp = -(-f // (2 * LANES_V7X)) * (2 * LANES_V7X)
    wg = _pad_cols(w_gate, fp).astype(BF16)[None]
    wu = _pad_cols(w_up, fp).astype(BF16)[None]
    wd = jnp.pad(w_down, ((0, fp - f), (0, 0))).astype(BF16)[None]
    g1 = ln_g[1].reshape(1, d_model)
    b1 = ln_b[1].reshape(1, d_model)
    return _ffn_ln(x1b, wg, wu, wd, x1, g1, b1, min(ROW_TILE, x.shape[0]), fp // 2,
                   "dense_swiglu_ln")


def _attn_moe_layer(x, xb, batch, seq, ln_g, ln_b, w_qkv, w_o, w_router, b_router,
                    w_gate, w_up, w_down):
    t, d_model = x.shape
    qscale = jnp.concatenate([jnp.full((d_model,), LOG2E / math.sqrt(SB_HEAD_DIM), F32),
                              jnp.ones((2 * d_model,), F32)])
    w_qkv_b = (w_qkv * qscale).astype(BF16)
    qkv = _matmul(xb, w_qkv_b, BF16, min(ROW_TILE * 2, t), d_model, "sb_qkv_proj")
    o = _attention(qkv, batch, seq, d_model)
    g0 = ln_g[0].reshape(1, d_model)
    b0 = ln_b[0].reshape(1, d_model)
    x1, _ = _matmul_ln(o, w_o.astype(BF16), x, g0, b0, min(ROW_TILE, t), "sb_out_proj_ln")

    info, cnt = _router(x1, w_router, b_router)
    counts = cnt[0, :N_EXPERTS].astype(jnp.int32)
    padded = (counts + MOE_ROWS - 1) // MOE_ROWS * MOE_ROWS
    pad_end = jnp.cumsum(padded)
    pad_start = pad_end - padded
    e1 = info[:, 0].astype(jnp.int32)
    e2 = info[:, 1].astype(jnp.int32)
    dest1 = pad_start[e1] + info[:, 4].astype(jnp.int32)
    dest2 = pad_start[e2] + info[:, 5].astype(jnp.int32)
    n_slots = 2 * t + N_EXPERTS * MOE_ROWS
    n_blocks = n_slots // MOE_ROWS
    block_start = jnp.arange(n_blocks) * MOE_ROWS
    block_expert = jnp.minimum(
        jnp.sum(pad_end[None, :] <= block_start[:, None], axis=1), N_EXPERTS - 1).astype(jnp.int32)
    n_used = (pad_end[-1] // MOE_ROWS).astype(jnp.int32).reshape(1)

    x_buf = _dispatch(x1, dest1, dest2, n_slots)
    f = w_gate.shape[2]
    y_buf = _ffn_experts(x_buf, block_expert, n_used, w_gate.astype(BF16), w_up.astype(BF16),
                         w_down.astype(BF16), MOE_ROWS, f // 4, "moe_expert_swiglu")
    g1 = ln_g[1].reshape(1, d_model)
    b1 = ln_b[1].reshape(1, d_model)
    return _combine_ln(y_buf, dest1, dest2, info, x1, g1, b1)


def kernel(x, ln_g, ln_b, ssm_w_in, ssm_conv_w, ssm_conv_b, ssm_dt_bias, ssm_a_log, ssm_d,
           ssm_norm_w, ssm_w_out, sb_w_qkv, sb_w_o, ffn_w_gate, ffn_w_up, ffn_w_down,
           moe_w_router, moe_b_router, moe_w_gate, moe_w_up, moe_w_down):
    batch, seq, d_model = x.shape
    xf = x.reshape(batch * seq, d_model)
    x1, x1b = _mamba_layer(xf, xf.astype(BF16), batch, seq, ln_g[0], ln_b[0], ssm_w_in[0],
                           ssm_conv_w[0], ssm_conv_b[0], ssm_dt_bias[0], ssm_a_log[0], ssm_d[0],
                           ssm_norm_w[0], ssm_w_out[0], ffn_w_gate[0], ffn_w_up[0], ffn_w_down[0])
    out = _attn_moe_layer(x1, x1b, batch, seq, ln_g[1], ln_b[1], sb_w_qkv[0], sb_w_o[0],
                          moe_w_router[0], moe_b_router[0], moe_w_gate[0], moe_w_up[0],
                          moe_w_down[0])
    return out.reshape(batch, seq, d_model)
```

```python
import functools
import math

import jax
import jax.numpy as jnp
from jax import lax
from jax.experimental import pallas as pl
from jax.experimental.pallas import tpu as pltpu

F32 = jnp.float32
BF16 = jnp.bfloat16

LANES_V7X = 128
SUBLANES_V7X = 8
VMEM_LIMIT_BYTES_V7X = 56 * 1024 * 1024

DEPTH = 2
SSM_HEAD_DIM = 64
SSM_GROUPS = 8
SSM_D_STATE = 128
SSM_CONV = 4
SSM_CHUNK = 128
SB_HEAD_DIM = 64
N_EXPERTS = 8
DEEPNORM_ALPHA = (2.0 * DEPTH) ** 0.25
LN_EPS = 1e-5
RMS_EPS = 1e-5
LOG2E = 1.4426950408889634
LN2 = 0.6931471805599453
F32_MIN_EXP2 = -150.0
SCORE_BOUND_SLACK = 1.01

ROW_TILE = 512
MOE_ROWS = 512
ATTN_Q_TILE = 256
ATTN_K_TILE = 128
ATTN_PAIRS_PER_STEP = 2
ROUTE_TILE = 1024


def _params(*semantics):
    return pltpu.CompilerParams(dimension_semantics=semantics,
                                vmem_limit_bytes=VMEM_LIMIT_BYTES_V7X)


def _layer_norm_rows(y, g, b):
    mu = jnp.mean(y, axis=-1, keepdims=True)
    d = y - mu
    var = jnp.mean(d * d, axis=-1, keepdims=True)
    return d * lax.rsqrt(var + LN_EPS) * g + b


def _silu(x):
    return x * (1.0 / (1.0 + jnp.exp(-x)))


def _softplus(x):
    return jnp.maximum(x, 0.0) + jnp.log(1.0 + jnp.exp(-jnp.abs(x)))


def _bdot(a, b):
    return jnp.dot(a.astype(BF16), b.astype(BF16), preferred_element_type=F32)


def _mm_kernel(x_ref, w_ref, o_ref):
    o_ref[...] = jnp.dot(x_ref[...], w_ref[...],
                         preferred_element_type=F32).astype(o_ref.dtype)


def _matmul(x, w, out_dtype, tm, tn, name):
    m, k = x.shape
    n = w.shape[1]
    return pl.pallas_call(
        _mm_kernel,
        grid=(m // tm, n // tn),
        in_specs=[pl.BlockSpec((tm, k), lambda i, j: (i, 0)),
                  pl.BlockSpec((k, tn), lambda i, j: (0, j))],
        out_specs=pl.BlockSpec((tm, tn), lambda i, j: (i, j)),
        out_shape=jax.ShapeDtypeStruct((m, n), out_dtype),
        compiler_params=_params("parallel", "arbitrary"),
        name=name,
    )(x, w)


def _mm_ln_kernel(a_ref, w_ref, res_ref, g_ref, b_ref, o_ref, ob_ref):
    h = jnp.dot(a_ref[...], w_ref[...], preferred_element_type=F32)
    o = _layer_norm_rows(DEEPNORM_ALPHA * res_ref[...] + h, g_ref[...], b_ref[...])
    o_ref[...] = o
    ob_ref[...] = o.astype(BF16)


def _matmul_ln(a, w, res, g, b, tm, name):
    m, k = a.shape
    n = w.shape[1]
    return pl.pallas_call(
        _mm_ln_kernel,
        grid=(m // tm,),
        in_specs=[pl.BlockSpec((tm, k), lambda i: (i, 0)),
                  pl.BlockSpec((k, n), lambda i: (0, 0)),
                  pl.BlockSpec((tm, n), lambda i: (i, 0)),
                  pl.BlockSpec((1, n), lambda i: (0, 0)),
                  pl.BlockSpec((1, n), lambda i: (0, 0))],
        out_specs=[pl.BlockSpec((tm, n), lambda i: (i, 0)),
                   pl.BlockSpec((tm, n), lambda i: (i, 0))],
        out_shape=[jax.ShapeDtypeStruct((m, n), F32),
                   jax.ShapeDtypeStruct((m, n), BF16)],
        compiler_params=_params("parallel"),
        name=name,
    )(a, w, res, g, b)


def _ffn_body(eid_ref, nused_ref, x_ref, wg_ref, wu_ref, wd_ref, acc_ref):
    j = pl.program_id(1)
    x = x_ref[...].astype(BF16)
    g = jnp.dot(x, wg_ref[0], preferred_element_type=F32)
    u = jnp.dot(x, wu_ref[0], preferred_element_type=F32)
    h = (_silu(g) * u).astype(BF16)
    part = jnp.dot(h, wd_ref[0], preferred_element_type=F32)

    @pl.when(j == 0)
    def _():
        acc_ref[...] = part

    @pl.when(j > 0)
    def _():
        acc_ref[...] += part


def _ffn_ln_kernel(eid_ref, nused_ref, x_ref, wg_ref, wu_ref, wd_ref, res_ref, g_ref, b_ref,
                   o_ref, ob_ref, acc_ref):
    _ffn_body(eid_ref, nused_ref, x_ref, wg_ref, wu_ref, wd_ref, acc_ref)

    @pl.when(pl.program_id(1) == pl.num_programs(1) - 1)
    def _():
        o = _layer_norm_rows(DEEPNORM_ALPHA * res_ref[...] + acc_ref[...],
                             g_ref[...], b_ref[...])
        o_ref[...] = o
        ob_ref[...] = o.astype(BF16)


def _ffn_raw_kernel(eid_ref, nused_ref, x_ref, wg_ref, wu_ref, wd_ref, o_ref, acc_ref):
    i = pl.program_id(0)
    last = pl.program_id(1) == pl.num_programs(1) - 1
    used = i < nused_ref[0]

    @pl.when(used)
    def _():
        _ffn_body(eid_ref, nused_ref, x_ref, wg_ref, wu_ref, wd_ref, acc_ref)

    @pl.when(jnp.logical_and(used, last))
    def _():
        o_ref[...] = acc_ref[...]

    @pl.when(jnp.logical_and(jnp.logical_not(used), last))
    def _():
        o_ref[...] = jnp.zeros_like(o_ref)


def _ffn_specs(tm, d, tf, nf):
    def wcol(i, j, eid, nused):
        return (eid[i], 0, jnp.where(i < nused[0], j, nf - 1))

    def wrow(i, j, eid, nused):
        return (eid[i], jnp.where(i < nused[0], j, nf - 1), 0)

    return [pl.BlockSpec((tm, d), lambda i, j, eid, nused: (i, 0)),
            pl.BlockSpec((1, d, tf), wcol),
            pl.BlockSpec((1, d, tf), wcol),
            pl.BlockSpec((1, tf, d), wrow)]


def _ffn_ln(xb, wg, wu, wd, res, g, b, tm, tf, name):
    m, d = xb.shape
    f = wg.shape[2]
    nf = f // tf
    nblk = m // tm
    eid = jnp.zeros((nblk,), jnp.int32)
    nused = jnp.full((1,), nblk, jnp.int32)
    row = lambda i, j, eid, nused: (i, 0)
    vec = lambda i, j, eid, nused: (0, 0)
    return pl.pallas_call(
        _ffn_ln_kernel,
        grid_spec=pltpu.PrefetchScalarGridSpec(
            num_scalar_prefetch=2,
            grid=(nblk, nf),
            in_specs=_ffn_specs(tm, d, tf, nf) + [
                pl.BlockSpec((tm, d), row),
                pl.BlockSpec((1, d), vec),
                pl.BlockSpec((1, d), vec)],
            out_specs=[pl.BlockSpec((tm, d), row), pl.BlockSpec((tm, d), row)],
            scratch_shapes=[pltpu.VMEM((tm, d), F32)]),
        out_shape=[jax.ShapeDtypeStruct((m, d), F32),
                   jax.ShapeDtypeStruct((m, d), BF16)],
        compiler_params=_params("parallel", "arbitrary"),
        name=name,
    )(eid, nused, xb, wg, wu, wd, res, g, b)


def _ffn_experts(x_buf, eid, nused, wg, wu, wd, tm, tf, name):
    m, d = x_buf.shape
    f = wg.shape[2]
    nf = f // tf
    nblk = m // tm
    row = lambda i, j, eid, nused: (i, 0)
    return pl.pallas_call(
        _ffn_raw_kernel,
        grid_spec=pltpu.PrefetchScalarGridSpec(
            num_scalar_prefetch=2,
            grid=(nblk, nf),
            in_specs=_ffn_specs(tm, d, tf, nf),
            out_specs=pl.BlockSpec((tm, d), row),
            scratch_shapes=[pltpu.VMEM((tm, d), F32)]),
        out_shape=jax.ShapeDtypeStruct((m, d), F32),
        compiler_params=_params("parallel", "arbitrary"),
        name=name,
    )(eid, nused, x_buf, wg, wu, wd)


def _split3(v):
    hi = v.astype(BF16)
    r = v - hi.astype(F32)
    mid = r.astype(BF16)
    lo = (r - mid.astype(F32)).astype(BF16)
    return hi, mid, lo


def _expand_cols(v, e):
    hi, mid, lo = _split3(v)
    out = jnp.dot(hi, e, preferred_element_type=F32)
    out += jnp.dot(mid, e, preferred_element_type=F32)
    out += jnp.dot(lo, e, preferred_element_type=F32)
    return out


def _ssd_kernel(xbc_ref, z_ref, dt_ref, convw_ref, convb_ref, dtb_ref, alog_ref, dexp_ref,
                normw_ref, ltri_ref, sel128_ref, sel64_ref, y_ref,
                ext_ref, act_ref, eoff_ref, state_ref, *, d_inner, n_heads):
    L = SSM_CHUNK
    N = SSM_D_STATE
    G = SSM_GROUPS
    hpg = n_heads // G
    gw = d_inner // G
    halo = SUBLANES_V7X

    @pl.when(pl.program_id(1) == 0)
    def _():
        ext_ref[0:halo, :] = jnp.zeros((halo, ext_ref.shape[1]), F32)
        state_ref[...] = jnp.zeros_like(state_ref)

    ext_ref[halo:halo + L, :] = xbc_ref[...]
    conv = convb_ref[...]
    for k in range(SSM_CONV):
        conv = conv + convw_ref[k:k + 1, :] * ext_ref[pl.ds(halo - (SSM_CONV - 1) + k, L), :]
    ext_ref[0:halo, :] = xbc_ref[L - halo:L, :]
    act_ref[...] = _silu(conv)

    dt = _softplus(dt_ref[...] + dtb_ref[...])
    a = -jnp.exp(alog_ref[...])
    acum = jnp.dot(ltri_ref[...], dt * a, preferred_element_type=F32,
                   precision=lax.Precision.HIGHEST)
    acum_last = acum[L - 1:L, :]
    acum_t = acum.T
    dt_t = dt.T
    eoff_ref[...] = _expand_cols(jnp.exp(acum), sel64_ref[...])
    eend = _expand_cols(jnp.exp(acum_last - acum) * dt, sel64_ref[...])

    row = lax.broadcasted_iota(jnp.int32, (L, L), 0)
    col = lax.broadcasted_iota(jnp.int32, (L, L), 1)
    causal = col <= row
    glane = lax.broadcasted_iota(jnp.int32, (L, gw), 1) // SSM_HEAD_DIM

    for g in range(G):
        xs_g = act_ref[:, g * gw:(g + 1) * gw]
        b_g = act_ref[:, d_inner + g * N:d_inner + (g + 1) * N]
        c_g = act_ref[:, d_inner + G * N + g * N:d_inner + G * N + (g + 1) * N]
        b_gt = b_g.T.astype(BF16)
        c_gb = c_g.astype(BF16)
        cb = jnp.dot(c_gb, b_gt, preferred_element_type=F32)
        acol = _expand_cols(acum, sel128_ref[:, g * hpg * L:(g + 1) * hpg * L])
        y_g = jnp.zeros((L, gw), F32)
        for h in range(hpg):
            hh = g * hpg + h
            seg = acol[:, h * L:(h + 1) * L] - acum_t[hh:hh + 1, :]
            decay = jnp.exp(jnp.where(causal, seg, -jnp.inf))
            w = cb * decay * dt_t[hh:hh + 1, :]
            xs_h = jnp.where(glane == h, xs_g, 0.0)
            y_g = y_g + _bdot(w, xs_h)
        state_g = state_ref[:, g * gw:(g + 1) * gw]
        eoff_g = eoff_ref[:, g * gw:(g + 1) * gw]
        y_g = y_g + _bdot(c_gb, state_g) * eoff_g
        xs_scaled = xs_g * eend[:, g * gw:(g + 1) * gw]
        state_ref[:, g * gw:(g + 1) * gw] = (
            state_g * eoff_g[L - 1:L, :]
            + jnp.dot(b_gt, xs_scaled.astype(BF16), preferred_element_type=F32))
        y_g = y_g + dexp_ref[:, g * gw:(g + 1) * gw] * xs_g
        y_g = y_g * _silu(z_ref[:, g * gw:(g + 1) * gw])
        ms = jnp.mean(y_g * y_g, axis=-1, keepdims=True)
        y_g = y_g * lax.rsqrt(ms + RMS_EPS) * normw_ref[:, g * gw:(g + 1) * gw]
        y_ref[:, g * gw:(g + 1) * gw] = y_g.astype(y_ref.dtype)


def _ssd(zx, conv_w, conv_b, dt_bias, a_log, d_skip, norm_w, batch, seq, d_inner, n_heads):
    L = SSM_CHUNK
    conv_dim = d_inner + 2 * SSM_GROUPS * SSM_D_STATE
    nc = seq // L
    pad = LANES_V7X - n_heads
    dtb = jnp.pad(dt_bias, (0, pad)).reshape(1, LANES_V7X)
    alog = jnp.pad(a_log, (0, pad)).reshape(1, LANES_V7X)
    dexp = jnp.repeat(d_skip, SSM_HEAD_DIM).reshape(1, d_inner)
    ltri = jnp.tril(jnp.ones((L, L), F32))
    hidx = jnp.arange(LANES_V7X)[:, None]
    sel128 = (hidx == (jnp.arange(n_heads * L) // L)[None, :]).astype(BF16)
    sel64 = (hidx == (jnp.arange(d_inner) // SSM_HEAD_DIM)[None, :]).astype(BF16)
    xbc_blocks = conv_dim // conv_dim
    del xbc_blocks
    z_blk = conv_dim // d_inner
    dt_blk = (conv_dim + d_inner) // LANES_V7X
    const = lambda b, c: (0, 0)
    kern = functools.partial(_ssd_kernel, d_inner=d_inner, n_heads=n_heads)
    return pl.pallas_call(
        kern,
        grid=(batch, nc),
        in_specs=[pl.BlockSpec((L, conv_dim), lambda b, c: (b * nc + c, 0)),
                  pl.BlockSpec((L, d_inner), lambda b, c: (b * nc + c, z_blk)),
                  pl.BlockSpec((L, LANES_V7X), lambda b, c: (b * nc + c, dt_blk)),
                  pl.BlockSpec((SSM_CONV, conv_dim), const),
                  pl.BlockSpec((1, conv_dim), const),
                  pl.BlockSpec((1, LANES_V7X), const),
                  pl.BlockSpec((1, LANES_V7X), const),
                  pl.BlockSpec((1, d_inner), const),
                  pl.BlockSpec((1, d_inner), const),
                  pl.BlockSpec((L, L), const),
                  pl.BlockSpec((LANES_V7X, n_heads * L), const),
                  pl.BlockSpec((LANES_V7X, d_inner), const)],
        out_specs=pl.BlockSpec((L, d_inner), lambda b, c: (b * nc + c, 0)),
        out_shape=jax.ShapeDtypeStruct((batch * seq, d_inner), BF16),
        scratch_shapes=[pltpu.VMEM((L + SUBLANES_V7X, conv_dim), F32),
                        pltpu.VMEM((L, conv_dim), F32),
                        pltpu.VMEM((L, d_inner), F32),
                        pltpu.VMEM((SSM_D_STATE, d_inner), F32)],
        compiler_params=_params("parallel", "arbitrary"),
        name="ssd_scan",
    )(zx, zx, zx, conv_w, conv_b.reshape(1, conv_dim), dtb, alog, dexp,
      norm_w.reshape(1, d_inner), ltri, sel128, sel64)


def _attn_kernel(q_ref, k_ref, v_ref, mbd_ref, o_ref, kk_ref, vv_ref, acc_ref, kn_ref, *, tq):
    tk = ATTN_K_TILE
    qi = pl.program_id(2)
    n_kb = k_ref.shape[0] // tk
    n_diag = tq // tk
    lane = lax.broadcasted_iota(jnp.int32, (tk, LANES_V7X), 1)
    head0 = lane < SB_HEAD_DIM

    n_pairs = q_ref.shape[1] // LANES_V7X
    rows = SUBLANES_V7X

    @pl.when(qi == 0)
    def _():
        for c in range(n_pairs):
            cols = slice(c * LANES_V7X, (c + 1) * LANES_V7X)

            def fill(j, carry, c=c, cols=cols):
                m0, m1 = carry
                kb = k_ref[pl.ds(pl.multiple_of(j * tk, tk), tk), cols]
                vb = v_ref[pl.ds(pl.multiple_of(j * tk, tk), tk), cols]
                zero = jnp.zeros_like(kb)
                kk_ref[c, j, 0:tk, :] = jnp.where(head0, kb, zero)
                kk_ref[c, j, tk:2 * tk, :] = jnp.where(head0, zero, kb)
                vv_ref[c, j, 0:tk, :] = jnp.where(head0, vb, zero)
                vv_ref[c, j, tk:2 * tk, :] = jnp.where(head0, zero, vb)
                ksq = kb.astype(F32) * kb.astype(F32)
                n0 = jnp.sum(jnp.where(head0, ksq, 0.0), axis=-1, keepdims=True)
                n1 = jnp.sum(jnp.where(head0, 0.0, ksq), axis=-1, keepdims=True)
                return jnp.maximum(m0, n0), jnp.maximum(m1, n1)
            init = (jnp.zeros((tk, 1), F32), jnp.zeros((tk, 1), F32))
            m0, m1 = lax.fori_loop(0, n_kb, fill, init)
            kn_ref[(2 * c) * rows:(2 * c + 1) * rows, :] = jnp.broadcast_to(
                jnp.max(m0, axis=0, keepdims=True), (rows, LANES_V7X))
            kn_ref[(2 * c + 1) * rows:(2 * c + 2) * rows, :] = jnp.broadcast_to(
                jnp.max(m1, axis=0, keepdims=True), (rows, LANES_V7X))

    acc_ref[...] = jnp.zeros_like(acc_ref)
    t_pos = qi * tq + lax.broadcasted_iota(jnp.int32, (tq, 2 * tk), 0)
    s_off = lax.broadcasted_iota(jnp.int32, (tq, 2 * tk), 1) % tk
    qhead0 = lax.broadcasted_iota(jnp.int32, (tq, LANES_V7X), 1) < SB_HEAD_DIM

    qs = []
    zbmax = []
    for c in range(n_pairs):
        q = q_ref[:, c * LANES_V7X:(c + 1) * LANES_V7X]
        qs.append(q)
        qsq = q.astype(F32) * q.astype(F32)
        for h, sel in enumerate((qhead0, jnp.logical_not(qhead0))):
            qn = jnp.max(jnp.sum(jnp.where(sel, qsq, 0.0), axis=-1, keepdims=True),
                         axis=0, keepdims=True)
            kn = kn_ref[(2 * c + h) * rows:(2 * c + h) * rows + 1, 0:1]
            zbmax.append(jnp.sqrt(qn * kn) * SCORE_BOUND_SLACK + SCORE_BOUND_SLACK)

    def step(c, j, carry, masked):
        a0, a1 = carry
        cols = slice(c * LANES_V7X, (c + 1) * LANES_V7X)
        z = lax.dot_general(qs[c], kk_ref[c, j], (((1,), (1,)), ((), ())),
                            preferred_element_type=F32)
        sp = jnp.maximum(z, 0.0) + jnp.log2(1.0 + jnp.exp2(-jnp.abs(z)))
        if masked:
            mask = (j * tk + s_off) < t_pos
            sp = jnp.where(mask, sp, 0.0)
        incl = jnp.dot(sp.astype(BF16), mbd_ref[...], preferred_element_type=F32)
        la0 = z[:, :tk] - incl[:, :tk] - a0
        la1 = z[:, tk:] - incl[:, tk:] - a1
        p = jnp.concatenate([jnp.exp2(la0), jnp.exp2(la1)], axis=1)
        if masked:
            p = jnp.where(mask, p, 0.0)
        acc_ref[:, cols] += jnp.dot(p.astype(BF16), vv_ref[c, j], preferred_element_type=F32)
        a0 = a0 + jnp.sum(sp[:, :tk], axis=-1, keepdims=True)
        a1 = a1 + jnp.sum(sp[:, tk:], axis=-1, keepdims=True)
        return a0, a1

    def live(accs):
        worst = zbmax[0] - jnp.min(accs[0], axis=0, keepdims=True)
        for zb, a in zip(zbmax[1:], accs[1:]):
            worst = jnp.maximum(worst, zb - jnp.min(a, axis=0, keepdims=True))
        return (jnp.max(worst) > F32_MIN_EXP2).astype(jnp.int32)

    def all_steps(j, accs, masked):
        out = []
        for c in range(n_pairs):
            out.extend(step(c, j, (accs[2 * c], accs[2 * c + 1]), masked))
        return tuple(out)

    accs = tuple(jnp.zeros((tq, 1), F32) for _ in range(2 * n_pairs))
    j_diag = qi * n_diag
    for d in range(n_diag - 1, -1, -1):
        accs = all_steps(j_diag + d, accs, True)

    def cond(carry):
        return jnp.logical_and(carry[0] < j_diag, carry[1] > 0)

    def body(carry):
        m = carry[0]
        accs = all_steps(j_diag - 1 - m, carry[2:], False)
        return (m + 1, live(accs)) + accs

    lax.while_loop(cond, body, (jnp.int32(0), live(accs)) + accs)
    o_ref[...] = acc_ref[...].astype(o_ref.dtype)


def _attention(qkv, batch, seq, d_model):
    tq = min(ATTN_Q_TILE, seq)
    tk = ATTN_K_TILE
    nq = seq // tq
    width = ATTN_PAIRS_PER_STEP * LANES_V7X
    n_groups = d_model // width
    r = jnp.arange(2 * tk)
    mbd = ((r[:, None] // tk == r[None, :] // tk) & (r[:, None] >= r[None, :])).astype(BF16)
    kern = functools.partial(_attn_kernel, tq=tq)
    return pl.pallas_call(
        kern,
        grid=(batch, n_groups, nq),
        in_specs=[pl.BlockSpec((tq, width), lambda b, p, i: (b * nq + i, p)),
                  pl.BlockSpec((seq, width), lambda b, p, i: (b, n_groups + p)),
                  pl.BlockSpec((seq, width), lambda b, p, i: (b, 2 * n_groups + p)),
                  pl.BlockSpec((2 * tk, 2 * tk), lambda b, p, i: (0, 0))],
        out_specs=pl.BlockSpec((tq, width), lambda b, p, i: (b * nq + i, p)),
        out_shape=jax.ShapeDtypeStruct((batch * seq, d_model), BF16),
        scratch_shapes=[pltpu.VMEM((ATTN_PAIRS_PER_STEP, seq // tk, 2 * tk, LANES_V7X), BF16),
                        pltpu.VMEM((ATTN_PAIRS_PER_STEP, seq // tk, 2 * tk, LANES_V7X), BF16),
                        pltpu.VMEM((tq, width), F32),
                        pltpu.VMEM((2 * ATTN_PAIRS_PER_STEP * SUBLANES_V7X, LANES_V7X), F32)],
        compiler_params=_params("parallel", "parallel", "arbitrary"),
        name="stickbreak_attn",
    )(qkv, qkv, qkv, mbd)


def _router_kernel(x_ref, w_ref, b_ref, ltri_ref, info_ref, cnt_ref, carry_ref):
    tm = x_ref.shape[0]

    @pl.when(pl.program_id(0) == 0)
    def _():
        carry_ref[...] = jnp.zeros_like(carry_ref)

    logits = jnp.dot(x_ref[...], w_ref[...], preferred_element_type=F32,
                     precision=lax.Precision.HIGHEST) + b_ref[...]
    lane = lax.broadcasted_iota(jnp.int32, (tm, LANES_V7X), 1).astype(F32)
    big = float(LANES_V7X)
    m1 = jnp.max(logits, axis=-1, keepdims=True)
    i1 = jnp.min(jnp.where(logits == m1, lane, big), axis=-1, keepdims=True)
    l2 = jnp.where(lane == i1, -jnp.inf, logits)
    m2 = jnp.max(l2, axis=-1, keepdims=True)
    i2 = jnp.min(jnp.where(l2 == m2, lane, big), axis=-1, keepdims=True)
    e = jnp.exp(m2 - m1)
    g1 = 1.0 / (1.0 + e)
    g2 = e * g1
    hit1 = lane == i1
    hit2 = lane == i2
    oh = jnp.where(jnp.logical_or(hit1, hit2), 1.0, 0.0)
    cum = jnp.dot(ltri_ref[...], oh.astype(BF16), preferred_element_type=F32)
    before = cum - oh + carry_ref[0:1, :]
    r1 = jnp.sum(jnp.where(hit1, before, 0.0), axis=-1, keepdims=True)
    r2 = jnp.sum(jnp.where(hit2, before, 0.0), axis=-1, keepdims=True)
    total = carry_ref[0:1, :] + cum[tm - 1:tm, :]
    carry_ref[...] = jnp.broadcast_to(total, carry_ref.shape)
    cnt_ref[...] = jnp.broadcast_to(total, cnt_ref.shape)
    info = jnp.where(lane == 0.0, i1, 0.0)
    info = jnp.where(lane == 1.0, i2, info)
    info = jnp.where(lane == 2.0, g1, info)
    info = jnp.where(lane == 3.0, g2, info)
    info = jnp.where(lane == 4.0, r1, info)
    info = jnp.where(lane == 5.0, r2, info)
    info_ref[...] = info


def _router(x, w_router, b_router):
    t, d = x.shape
    tm = min(ROUTE_TILE, t)
    pad = LANES_V7X - N_EXPERTS
    w = jnp.pad(w_router, ((0, 0), (0, pad)))
    b = jnp.concatenate([b_router.astype(F32), jnp.full((pad,), -1e30, F32)]).reshape(1, LANES_V7X)
    ltri = jnp.tril(jnp.ones((tm, tm), BF16))
    return pl.pallas_call(
        _router_kernel,
        grid=(t // tm,),
        in_specs=[pl.BlockSpec((tm, d), lambda i: (i, 0)),
                  pl.BlockSpec((d, LANES_V7X), lambda i: (0, 0)),
                  pl.BlockSpec((1, LANES_V7X), lambda i: (0, 0)),
                  pl.BlockSpec((tm, tm), lambda i: (0, 0))],
        out_specs=[pl.BlockSpec((tm, LANES_V7X), lambda i: (i, 0)),
                   pl.BlockSpec((SUBLANES_V7X, LANES_V7X), lambda i: (0, 0))],
        out_shape=[jax.ShapeDtypeStruct((t, LANES_V7X), F32),
                   jax.ShapeDtypeStruct((SUBLANES_V7X, LANES_V7X), F32)],
        scratch_shapes=[pltpu.VMEM((SUBLANES_V7X, LANES_V7X), F32)],
        compiler_params=_params("arbitrary"),
        name="moe_router",
    )(x, w, b, ltri)


def _row_copy(src_ref, src_row, dst_ref, dst_row, sem):
    return pltpu.make_async_copy(src_ref.at[pl.ds(src_row, 1)],
                                 dst_ref.at[pl.ds(dst_row, 1)], sem)


def _dispatch_kernel(d1_ref, d2_ref, x_ref, buf_in_hbm, buf_hbm, sem):
    del buf_in_hbm
    tm = d1_ref.shape[0]

    def issue(t, c):
        _row_copy(x_ref, t, buf_hbm, d1_ref[t], sem).start()
        _row_copy(x_ref, t, buf_hbm, d2_ref[t], sem).start()
        return c
    lax.fori_loop(0, tm, issue, 0, unroll=8)

    def drain(t, c):
        _row_copy(x_ref, 0, buf_hbm, 0, sem).wait()
        _row_copy(x_ref, 0, buf_hbm, 0, sem).wait()
        return c
    lax.fori_loop(0, tm, drain, 0, unroll=8)


def _dispatch(x, dest1, dest2, n_slots):
    t, d = x.shape
    tm = min(ROUTE_TILE, t)
    buf0 = jnp.zeros((n_slots, d), x.dtype)
    smem = lambda: pl.BlockSpec((tm,), lambda i: (i,), memory_space=pltpu.SMEM)
    return pl.pallas_call(
        _dispatch_kernel,
        grid=(t // tm,),
        in_specs=[smem(), smem(),
                  pl.BlockSpec((tm, d), lambda i: (i, 0)),
                  pl.BlockSpec(memory_space=pl.ANY)],
        out_specs=pl.BlockSpec(memory_space=pl.ANY),
        out_shape=jax.ShapeDtypeStruct((n_slots, d), x.dtype),
        scratch_shapes=[pltpu.SemaphoreType.DMA],
        input_output_aliases={3: 0},
        compiler_params=_params("arbitrary"),
        name="moe_dispatch",
    )(dest1, dest2, x, buf0)


def _combine_kernel(d1_ref, d2_ref, info_ref, y_hbm, res_ref, g_ref, b_ref, o_ref,
                    buf1_ref, buf2_ref, sem):
    tm = d1_ref.shape[0]

    def issue(t, c):
        _row_copy(y_hbm, d1_ref[t], buf1_ref, t, sem).start()
        _row_copy(y_hbm, d2_ref[t], buf2_ref, t, sem).start()
        return c
    lax.fori_loop(0, tm, issue, 0, unroll=8)

    def drain(t, c):
        _row_copy(y_hbm, 0, buf1_ref, 0, sem).wait()
        _row_copy(y_hbm, 0, buf2_ref, 0, sem).wait()
        return c
    lax.fori_loop(0, tm, drain, 0, unroll=8)

    g1 = info_ref[:, 2:3]
    g2 = info_ref[:, 3:4]
    y = g1 * buf1_ref[...] + g2 * buf2_ref[...]
    o_ref[...] = _layer_norm_rows(DEEPNORM_ALPHA * res_ref[...] + y, g_ref[...], b_ref[...])


def _combine_ln(y_buf, dest1, dest2, info, res, g, b):
    t, d = res.shape
    tm = min(ROUTE_TILE, t)
    smem = lambda: pl.BlockSpec((tm,), lambda i: (i,), memory_space=pltpu.SMEM)
    return pl.pallas_call(
        _combine_kernel,
        grid=(t // tm,),
        in_specs=[smem(), smem(),
                  pl.BlockSpec((tm, LANES_V7X), lambda i: (i, 0)),
                  pl.BlockSpec(memory_space=pl.ANY),
                  pl.BlockSpec((tm, d), lambda i: (i, 0)),
                  pl.BlockSpec((1, d), lambda i: (0, 0)),
                  pl.BlockSpec((1, d), lambda i: (0, 0))],
        out_specs=pl.BlockSpec((tm, d), lambda i: (i, 0)),
        out_shape=jax.ShapeDtypeStruct((t, d), F32),
        scratch_shapes=[pltpu.VMEM((tm, d), F32), pltpu.VMEM((tm, d), F32),
                        pltpu.SemaphoreType.DMA],
        compiler_params=_params("arbitrary"),
        name="moe_combine_ln",
    )(dest1, dest2, info, y_buf, res, g, b)


def _pad_cols(w, n):
    return jnp.pad(w, ((0, 0), (0, n - w.shape[1])))


def _mamba_layer(x, xb, batch, seq, ln_g, ln_b, w_in, conv_w, conv_b, dt_bias, a_log, d_skip,
                 norm_w, w_out, w_gate, w_up, w_down):
    d_model = x.shape[1]
    n_heads = a_log.shape[0]
    d_inner = n_heads * SSM_HEAD_DIM
    conv_dim = conv_w.shape[1]
    w_z, w_xbc, w_dt = jnp.split(w_in, [d_inner, d_inner + conv_dim], axis=1)
    w_in_r = jnp.concatenate([w_xbc, w_z, _pad_cols(w_dt, LANES_V7X)], axis=1).astype(BF16)
    n_in = w_in_r.shape[1]
    tn = n_in // 7 if n_in % (7 * LANES_V7X) == 0 else LANES_V7X
    zx = _matmul(xb, w_in_r, F32, min(ROW_TILE * 2, x.shape[0]), tn, "ssm_in_proj")
    y = _ssd(zx, conv_w, conv_b, dt_bias, a_log, d_skip, norm_w, batch, seq, d_inner, n_heads)
    g0 = ln_g[0].reshape(1, d_model)
    b0 = ln_b[0].reshape(1, d_model)
    x1, x1b = _matmul_ln(y, w_out.astype(BF16), x, g0, b0, min(ROW_TILE, x.shape[0]),
                         "ssm_out_proj_ln")
    f = w_gate.shape[1]
    fp = -(-f // (2 * LANES_V7X)) * (2 * LANES_V7X)
    wg = _pad_cols(w_gate, fp).astype(BF16)[None]
    wu = _pad_cols(w_up, fp).astype(BF16)[None]
    wd = jnp.pad(w_down, ((0, fp - f), (0, 0))).astype(BF16)[None]
    g1 = ln_g[1].reshape(1, d_model)
    b1 = ln_b[1].reshape(1, d_model)
    return _ffn_ln(x1b, wg, wu, wd, x1, g1, b1, min(ROW_TILE, x.shape[0]), fp // 2,
                   "dense_swiglu_ln")


def _attn_moe_layer(x, xb, batch, seq, ln_g, ln_b, w_qkv, w_o, w_router, b_router,
                    w_gate, w_up, w_down):
    t, d_model = x.shape
    qscale = jnp.concatenate([jnp.full((d_model,), LOG2E / math.sqrt(SB_HEAD_DIM), F32),
                              jnp.ones((2 * d_model,), F32)])
    w_qkv_b = (w_qkv * qscale).astype(BF16)
    qkv = _matmul(xb, w_qkv_b, BF16, min(ROW_TILE * 2, t), d_model, "sb_qkv_proj")
    o = _attention(qkv, batch, seq, d_model)
    g0 = ln_g[0].reshape(1, d_model)
    b0 = ln_b[0].reshape(1, d_model)
    x1, _ = _matmul_ln(o, w_o.astype(BF16), x, g0, b0, min(ROW_TILE, t), "sb_out_proj_ln")

    info, cnt = _router(x1, w_router, b_router)
    counts = cnt[0, :N_EXPERTS].astype(jnp.int32)
    padded = (counts + MOE_ROWS - 1) // MOE_ROWS * MOE_ROWS
    pad_end = jnp.cumsum(padded)
    pad_start = pad_end - padded
    e1 = info[:, 0].astype(jnp.int32)
    e2 = info[:, 1].astype(jnp.int32)
    dest1 = pad_start[e1] + info[:, 4].astype(jnp.int32)
    dest2 = pad_start[e2] + info[:, 5].astype(jnp.int32)
    n_slots = 2 * t + N_EXPERTS * MOE_ROWS
    n_blocks = n_slots // MOE_ROWS
    block_start = jnp.arange(n_blocks) * MOE_ROWS
    block_expert = jnp.minimum(
        jnp.sum(pad_end[None, :] <= block_start[:, None], axis=1), N_EXPERTS - 1).astype(jnp.int32)
    n_used = (pad_end[-1] // MOE_ROWS).astype(jnp.int32).reshape(1)

    x_buf = _dispatch(x1, dest1, dest2, n_slots)
    f = w_gate.shape[2]
    y_buf = _ffn_experts(x_buf, block_expert, n_used, w_gate.astype(BF16), w_up.astype(BF16),
                         w_down.astype(BF16), MOE_ROWS, f // 4, "moe_expert_swiglu")
    g1 = ln_g[1].reshape(1, d_model)
    b1 = ln_b[1].reshape(1, d_model)
    return _combine_ln(y_buf, dest1, dest2, info, x1, g1, b1)


def kernel(x, ln_g, ln_b, ssm_w_in, ssm_conv_w, ssm_conv_b, ssm_dt_bias, ssm_a_log, ssm_d,
           ssm_norm_w, ssm_w_out, sb_w_qkv, sb_w_o, ffn_w_gate, ffn_w_up, ffn_w_down,
           moe_w_router, moe_b_router, moe_w_gate, moe_w_up, moe_w_down):
    batch, seq, d_model = x.shape
    xf = x.reshape(batch * seq, d_model)
    x1, x1b = _mamba_layer(xf, xf.astype(BF16), batch, seq, ln_g[0], ln_b[0], ssm_w_in[0],
                           ssm_conv_w[0], ssm_conv_b[0], ssm_dt_bias[0], ssm_a_log[0], ssm_d[0],
                           ssm_norm_w[0], ssm_w_out[0], ffn_w_gate[0], ffn_w_up[0], ffn_w_down[0])
    out = _attn_moe_layer(x1, x1b, batch, seq, ln_g[1], ln_b[1], sb_w_qkv[0], sb_w_o[0],
                          moe_w_router[0], moe_b_router[0], moe_w_gate[0], moe_w_up[0],
                          moe_w_down[0])
    return out.reshape(batch, seq, d_model)
```

```python
import functools
import math

import jax
import jax.numpy as jnp
from jax import lax
from jax.experimental import pallas as pl
from jax.experimental.pallas import tpu as pltpu

F32 = jnp.float32
BF16 = jnp.bfloat16

LANES_V7X = 128
SUBLANES_V7X = 8
VMEM_LIMIT_BYTES_V7X = 56 * 1024 * 1024

DEPTH = 2
SSM_HEAD_DIM = 64
SSM_GROUPS = 8
SSM_D_STATE = 128
SSM_CONV = 4
SSM_CHUNK = 128
SB_HEAD_DIM = 64
N_EXPERTS = 8
DEEPNORM_ALPHA = (2.0 * DEPTH) ** 0.25
LN_EPS = 1e-5
RMS_EPS = 1e-5
LOG2E = 1.4426950408889634
LN2 = 0.6931471805599453
F32_MIN_EXP2 = -150.0
SCORE_BOUND_SLACK = 1.01

ROW_TILE = 512
MOE_ROWS = 512
ATTN_Q_TILE = 256
ATTN_K_TILE = 128
ATTN_PAIRS_PER_STEP = 2
ATTN_SUB_ROWS = 128
ROUTE_TILE = 1024


def _params(*semantics):
    return pltpu.CompilerParams(dimension_semantics=semantics,
                                vmem_limit_bytes=VMEM_LIMIT_BYTES_V7X)


def _layer_norm_rows(y, g, b):
    mu = jnp.mean(y, axis=-1, keepdims=True)
    d = y - mu
    var = jnp.mean(d * d, axis=-1, keepdims=True)
    return d * lax.rsqrt(var + LN_EPS) * g + b


def _silu(x):
    return x * (1.0 / (1.0 + jnp.exp(-x)))


def _softplus(x):
    return jnp.maximum(x, 0.0) + jnp.log(1.0 + jnp.exp(-jnp.abs(x)))


def _bdot(a, b):
    return jnp.dot(a.astype(BF16), b.astype(BF16), preferred_element_type=F32)


def _mm_kernel(x_ref, w_ref, o_ref):
    o_ref[...] = jnp.dot(x_ref[...], w_ref[...],
                         preferred_element_type=F32).astype(o_ref.dtype)


def _matmul(x, w, out_dtype, tm, tn, name):
    m, k = x.shape
    n = w.shape[1]
    return pl.pallas_call(
        _mm_kernel,
        grid=(m // tm, n // tn),
        in_specs=[pl.BlockSpec((tm, k), lambda i, j: (i, 0)),
                  pl.BlockSpec((k, tn), lambda i, j: (0, j))],
        out_specs=pl.BlockSpec((tm, tn), lambda i, j: (i, j)),
        out_shape=jax.ShapeDtypeStruct((m, n), out_dtype),
        compiler_params=_params("parallel", "arbitrary"),
        name=name,
    )(x, w)


def _mm_resident_kernel(x_ref, w_ref, o_ref, *, chunk):
    x = x_ref[...]
    for c in range(o_ref.shape[1] // chunk):
        cols = slice(c * chunk, (c + 1) * chunk)
        o_ref[:, cols] = jnp.dot(x, w_ref[:, cols],
                                 preferred_element_type=F32).astype(o_ref.dtype)


def _matmul_resident(x, w, out_dtype, tm, chunk, name):
    m, k = x.shape
    n = w.shape[1]
    return pl.pallas_call(
        functools.partial(_mm_resident_kernel, chunk=chunk),
        grid=(m // tm,),
        in_specs=[pl.BlockSpec((tm, k), lambda i: (i, 0)),
                  pl.BlockSpec((k, n), lambda i: (0, 0), pipeline_mode=pl.Buffered(1))],
        out_specs=pl.BlockSpec((tm, n), lambda i: (i, 0)),
        out_shape=jax.ShapeDtypeStruct((m, n), out_dtype),
        compiler_params=_params("parallel"),
        name=name,
    )(x, w)


def _mm_ln_kernel(a_ref, w_ref, res_ref, g_ref, b_ref, o_ref, ob_ref):
    h = jnp.dot(a_ref[...], w_ref[...], preferred_element_type=F32)
    o = _layer_norm_rows(DEEPNORM_ALPHA * res_ref[...] + h, g_ref[...], b_ref[...])
    o_ref[...] = o
    ob_ref[...] = o.astype(BF16)


def _matmul_ln(a, w, res, g, b, tm, name):
    m, k = a.shape
    n = w.shape[1]
    return pl.pallas_call(
        _mm_ln_kernel,
        grid=(m // tm,),
        in_specs=[pl.BlockSpec((tm, k), lambda i: (i, 0)),
                  pl.BlockSpec((k, n), lambda i: (0, 0)),
                  pl.BlockSpec((tm, n), lambda i: (i, 0)),
                  pl.BlockSpec((1, n), lambda i: (0, 0)),
                  pl.BlockSpec((1, n), lambda i: (0, 0))],
        out_specs=[pl.BlockSpec((tm, n), lambda i: (i, 0)),
                   pl.BlockSpec((tm, n), lambda i: (i, 0))],
        out_shape=[jax.ShapeDtypeStruct((m, n), F32),
                   jax.ShapeDtypeStruct((m, n), BF16)],
        compiler_params=_params("parallel"),
        name=name,
    )(a, w, res, g, b)


def _ffn_body(eid_ref, nused_ref, x_ref, wg_ref, wu_ref, wd_ref, acc_ref):
    j = pl.program_id(1)
    x = x_ref[...].astype(BF16)
    g = jnp.dot(x, wg_ref[0], preferred_element_type=F32)
    u = jnp.dot(x, wu_ref[0], preferred_element_type=F32)
    h = (_silu(g) * u).astype(BF16)
    part = jnp.dot(h, wd_ref[0], preferred_element_type=F32)

    @pl.when(j == 0)
    def _():
        acc_ref[...] = part

    @pl.when(j > 0)
    def _():
        acc_ref[...] += part


def _ffn_ln_kernel(eid_ref, nused_ref, x_ref, wg_ref, wu_ref, wd_ref, res_ref, g_ref, b_ref,
                   o_ref, ob_ref, acc_ref):
    _ffn_body(eid_ref, nused_ref, x_ref, wg_ref, wu_ref, wd_ref, acc_ref)

    @pl.when(pl.program_id(1) == pl.num_programs(1) - 1)
    def _():
        o = _layer_norm_rows(DEEPNORM_ALPHA * res_ref[...] + acc_ref[...],
                             g_ref[...], b_ref[...])
        o_ref[...] = o
        ob_ref[...] = o.astype(BF16)


def _ffn_raw_kernel(eid_ref, nused_ref, x_ref, wg_ref, wu_ref, wd_ref, o_ref, acc_ref):
    i = pl.program_id(0)
    last = pl.program_id(1) == pl.num_programs(1) - 1
    used = i < nused_ref[0]

    @pl.when(used)
    def _():
        _ffn_body(eid_ref, nused_ref, x_ref, wg_ref, wu_ref, wd_ref, acc_ref)

    @pl.when(jnp.logical_and(used, last))
    def _():
        o_ref[...] = acc_ref[...]

    @pl.when(jnp.logical_and(jnp.logical_not(used), last))
    def _():
        o_ref[...] = jnp.zeros_like(o_ref)


def _ffn_specs(tm, d, tf, nf, fixed_weights=False):
    def wcol(i, j, eid, nused):
        return (eid[i], 0, jnp.where(i < nused[0], j, nf - 1))

    def wrow(i, j, eid, nused):
        return (eid[i], jnp.where(i < nused[0], j, nf - 1), 0)

    mode = dict(pipeline_mode=pl.Buffered(1)) if fixed_weights else {}
    return [pl.BlockSpec((tm, d), lambda i, j, eid, nused: (i, 0)),
            pl.BlockSpec((1, d, tf), wcol, **mode),
            pl.BlockSpec((1, d, tf), wcol, **mode),
            pl.BlockSpec((1, tf, d), wrow, **mode)]


def _ffn_ln(xb, wg, wu, wd, res, g, b, tm, tf, name):
    m, d = xb.shape
    f = wg.shape[2]
    nf = f // tf
    nblk = m // tm
    eid = jnp.zeros((nblk,), jnp.int32)
    nused = jnp.full((1,), nblk, jnp.int32)
    row = lambda i, j, eid, nused: (i, 0)
    vec = lambda i, j, eid, nused: (0, 0)
    return pl.pallas_call(
        _ffn_ln_kernel,
        grid_spec=pltpu.PrefetchScalarGridSpec(
            num_scalar_prefetch=2,
            grid=(nblk, nf),
            in_specs=_ffn_specs(tm, d, tf, nf, fixed_weights=(nf == 1)) + [
                pl.BlockSpec((tm, d), row),
                pl.BlockSpec((1, d), vec),
                pl.BlockSpec((1, d), vec)],
            out_specs=[pl.BlockSpec((tm, d), row), pl.BlockSpec((tm, d), row)],
            scratch_shapes=[pltpu.VMEM((tm, d), F32)]),
        out_shape=[jax.ShapeDtypeStruct((m, d), F32),
                   jax.ShapeDtypeStruct((m, d), BF16)],
        compiler_params=_params("parallel", "arbitrary"),
        name=name,
    )(eid, nused, xb, wg, wu, wd, res, g, b)


def _ffn_experts(x_buf, eid, nused, wg, wu, wd, tm, tf, name):
    m, d = x_buf.shape
    f = wg.shape[2]
    nf = f // tf
    nblk = m // tm
    row = lambda i, j, eid, nused: (i, 0)
    return pl.pallas_call(
        _ffn_raw_kernel,
        grid_spec=pltpu.PrefetchScalarGridSpec(
            num_scalar_prefetch=2,
            grid=(nblk, nf),
            in_specs=_ffn_specs(tm, d, tf, nf),
            out_specs=pl.BlockSpec((tm, d), row),
            scratch_shapes=[pltpu.VMEM((tm, d), F32)]),
        out_shape=jax.ShapeDtypeStruct((m, d), F32),
        compiler_params=_params("parallel", "arbitrary"),
        name=name,
    )(eid, nused, x_buf, wg, wu, wd)


def _split3(v):
    hi = v.astype(BF16)
    r = v - hi.astype(F32)
    mid = r.astype(BF16)
    lo = (r - mid.astype(F32)).astype(BF16)
    return hi, mid, lo


def _expand_cols(v, e):
    hi, mid, lo = _split3(v)
    out = jnp.dot(hi, e, preferred_element_type=F32)
    out += jnp.dot(mid, e, preferred_element_type=F32)
    out += jnp.dot(lo, e, preferred_element_type=F32)
    return out


def _ssd_kernel(xbc_ref, z_ref, dt_ref, convw_ref, convb_ref, dtb_ref, alog_ref, dexp_ref,
                normw_ref, ltri_ref, sel128_ref, sel64_ref, y_ref,
                ext_ref, act_ref, eoff_ref, state_ref, *, d_inner, n_heads):
    L = SSM_CHUNK
    N = SSM_D_STATE
    G = SSM_GROUPS
    hpg = n_heads // G
    gw = d_inner // G
    halo = SUBLANES_V7X

    @pl.when(pl.program_id(1) == 0)
    def _():
        ext_ref[0:halo, :] = jnp.zeros((halo, ext_ref.shape[1]), F32)
        state_ref[...] = jnp.zeros_like(state_ref)

    ext_ref[halo:halo + L, :] = xbc_ref[...].astype(F32)
    conv = convb_ref[...]
    for k in range(SSM_CONV):
        conv = conv + convw_ref[k:k + 1, :] * ext_ref[pl.ds(halo - (SSM_CONV - 1) + k, L), :]
    ext_ref[0:halo, :] = ext_ref[L:L + halo, :]
    act_ref[...] = _silu(conv)

    dt = _softplus(dt_ref[...] + dtb_ref[...])
    a = -jnp.exp(alog_ref[...])
    acum = jnp.dot(ltri_ref[...], dt * a, preferred_element_type=F32,
                   precision=lax.Precision.HIGHEST)
    acum_last = acum[L - 1:L, :]
    acum_t = acum.T
    dt_t = dt.T
    eoff_ref[...] = _expand_cols(jnp.exp(acum), sel64_ref[...])
    eend = _expand_cols(jnp.exp(acum_last - acum) * dt, sel64_ref[...])

    row = lax.broadcasted_iota(jnp.int32, (L, L), 0)
    col = lax.broadcasted_iota(jnp.int32, (L, L), 1)
    causal = col <= row
    glane = lax.broadcasted_iota(jnp.int32, (L, gw), 1) // SSM_HEAD_DIM

    for g in range(G):
        xs_g = act_ref[:, g * gw:(g + 1) * gw]
        b_g = act_ref[:, d_inner + g * N:d_inner + (g + 1) * N]
        c_g = act_ref[:, d_inner + G * N + g * N:d_inner + G * N + (g + 1) * N]
        b_gt = b_g.T.astype(BF16)
        c_gb = c_g.astype(BF16)
        cb = jnp.dot(c_gb, b_gt, preferred_element_type=F32)
        acol = _expand_cols(acum, sel128_ref[:, g * hpg * L:(g + 1) * hpg * L])
        y_g = jnp.zeros((L, gw), F32)
        for h in range(hpg):
            hh = g * hpg + h
            seg = acol[:, h * L:(h + 1) * L] - acum_t[hh:hh + 1, :]
            decay = jnp.exp(jnp.where(causal, seg, -jnp.inf))
            w = cb * decay * dt_t[hh:hh + 1, :]
            xs_h = jnp.where(glane == h, xs_g, 0.0)
            y_g = y_g + _bdot(w, xs_h)
        state_g = state_ref[:, g * gw:(g + 1) * gw]
        eoff_g = eoff_ref[:, g * gw:(g + 1) * gw]
        y_g = y_g + _bdot(c_gb, state_g) * eoff_g
        xs_scaled = xs_g * eend[:, g * gw:(g + 1) * gw]
        state_ref[:, g * gw:(g + 1) * gw] = (
            state_g * eoff_g[L - 1:L, :]
            + jnp.dot(b_gt, xs_scaled.astype(BF16), preferred_element_type=F32))
        y_g = y_g + dexp_ref[:, g * gw:(g + 1) * gw] * xs_g
        y_g = y_g * _silu(z_ref[:, g * gw:(g + 1) * gw].astype(F32))
        ms = jnp.mean(y_g * y_g, axis=-1, keepdims=True)
        y_g = y_g * lax.rsqrt(ms + RMS_EPS) * normw_ref[:, g * gw:(g + 1) * gw]
        y_ref[:, g * gw:(g + 1) * gw] = y_g.astype(y_ref.dtype)


def _ssd(zx, dt_raw, conv_w, conv_b, dt_bias, a_log, d_skip, norm_w, batch, seq, d_inner,
         n_heads):
    L = SSM_CHUNK
    conv_dim = d_inner + 2 * SSM_GROUPS * SSM_D_STATE
    nc = seq // L
    pad = LANES_V7X - n_heads
    dtb = jnp.pad(dt_bias, (0, pad)).reshape(1, LANES_V7X)
    alog = jnp.pad(a_log, (0, pad)).reshape(1, LANES_V7X)
    dexp = jnp.repeat(d_skip, SSM_HEAD_DIM).reshape(1, d_inner)
    ltri = jnp.tril(jnp.ones((L, L), F32))
    hidx = jnp.arange(LANES_V7X)[:, None]
    sel128 = (hidx == (jnp.arange(n_heads * L) // L)[None, :]).astype(BF16)
    sel64 = (hidx == (jnp.arange(d_inner) // SSM_HEAD_DIM)[None, :]).astype(BF16)
    z_blk = conv_dim // d_inner
    const = lambda b, c: (0, 0)
    kern = functools.partial(_ssd_kernel, d_inner=d_inner, n_heads=n_heads)
    return pl.pallas_call(
        kern,
        grid=(batch, nc),
        in_specs=[pl.BlockSpec((L, conv_dim), lambda b, c: (b * nc + c, 0)),
                  pl.BlockSpec((L, d_inner), lambda b, c: (b * nc + c, z_blk)),
                  pl.BlockSpec((L, LANES_V7X), lambda b, c: (b * nc + c, 0)),
                  pl.BlockSpec((SSM_CONV, conv_dim), const),
                  pl.BlockSpec((1, conv_dim), const),
                  pl.BlockSpec((1, LANES_V7X), const),
                  pl.BlockSpec((1, LANES_V7X), const),
                  pl.BlockSpec((1, d_inner), const),
                  pl.BlockSpec((1, d_inner), const),
                  pl.BlockSpec((L, L), const),
                  pl.BlockSpec((LANES_V7X, n_heads * L), const),
                  pl.BlockSpec((LANES_V7X, d_inner), const)],
        out_specs=pl.BlockSpec((L, d_inner), lambda b, c: (b * nc + c, 0)),
        out_shape=jax.ShapeDtypeStruct((batch * seq, d_inner), BF16),
        scratch_shapes=[pltpu.VMEM((L + SUBLANES_V7X, conv_dim), F32),
                        pltpu.VMEM((L, conv_dim), F32),
                        pltpu.VMEM((L, d_inner), F32),
                        pltpu.VMEM((SSM_D_STATE, d_inner), F32)],
        compiler_params=_params("parallel", "arbitrary"),
        name="ssd_scan",
    )(zx, zx, dt_raw, conv_w, conv_b.reshape(1, conv_dim), dtb, alog, dexp,
      norm_w.reshape(1, d_inner), ltri, sel128, sel64)


def _attn_kernel(q_ref, k_ref, v_ref, mbd_ref, o_ref, kk_ref, vv_ref, acc_ref, kn_ref, z_ref,
                 *, tq):
    tk = ATTN_K_TILE
    qi = pl.program_id(2)
    n_kb = k_ref.shape[0] // tk
    n_diag = tq // tk
    lane = lax.broadcasted_iota(jnp.int32, (tk, LANES_V7X), 1)
    head0 = lane < SB_HEAD_DIM

    n_pairs = q_ref.shape[1] // LANES_V7X
    rows = SUBLANES_V7X

    @pl.when(qi == 0)
    def _():
        for c in range(n_pairs):
            cols = slice(c * LANES_V7X, (c + 1) * LANES_V7X)

            def fill(j, carry, c=c, cols=cols):
                m0, m1 = carry
                kb = k_ref[pl.ds(pl.multiple_of(j * tk, tk), tk), cols]
                vb = v_ref[pl.ds(pl.multiple_of(j * tk, tk), tk), cols]
                zero = jnp.zeros_like(kb)
                kk_ref[c, j, 0:tk, :] = jnp.where(head0, kb, zero)
                kk_ref[c, j, tk:2 * tk, :] = jnp.where(head0, zero, kb)
                vv_ref[c, j, 0:tk, :] = jnp.where(head0, vb, zero)
                vv_ref[c, j, tk:2 * tk, :] = jnp.where(head0, zero, vb)
                ksq = kb.astype(F32) * kb.astype(F32)
                n0 = jnp.sum(jnp.where(head0, ksq, 0.0), axis=-1, keepdims=True)
                n1 = jnp.sum(jnp.where(head0, 0.0, ksq), axis=-1, keepdims=True)
                return jnp.maximum(m0, n0), jnp.maximum(m1, n1)
            init = (jnp.zeros((tk, 1), F32), jnp.zeros((tk, 1), F32))
            m0, m1 = lax.fori_loop(0, n_kb, fill, init, unroll=4)
            kn_ref[(2 * c) * rows:(2 * c + 1) * rows, :] = jnp.broadcast_to(
                jnp.max(m0, axis=0, keepdims=True), (rows, LANES_V7X))
            kn_ref[(2 * c + 1) * rows:(2 * c + 2) * rows, :] = jnp.broadcast_to(
                jnp.max(m1, axis=0, keepdims=True), (rows, LANES_V7X))

    acc_ref[...] = jnp.zeros_like(acc_ref)
    sub = min(ATTN_SUB_ROWS, tq)
    n_sub = tq // sub
    s_off = lax.broadcasted_iota(jnp.int32, (sub, 2 * tk), 1) % tk
    row = lax.broadcasted_iota(jnp.int32, (sub, 2 * tk), 0)
    qhead0 = lax.broadcasted_iota(jnp.int32, (tq, LANES_V7X), 1) < SB_HEAD_DIM

    chains = [(c, s) for c in range(n_pairs) for s in range(n_sub)]
    qs = {}
    zbmax = []
    for c in range(n_pairs):
        q = q_ref[:, c * LANES_V7X:(c + 1) * LANES_V7X]
        qsq = q.astype(F32) * q.astype(F32)
        zb_pair = []
        for h, sel in enumerate((qhead0, jnp.logical_not(qhead0))):
            qn = jnp.max(jnp.sum(jnp.where(sel, qsq, 0.0), axis=-1, keepdims=True),
                         axis=0, keepdims=True)
            kn = kn_ref[(2 * c + h) * rows:(2 * c + h) * rows + 1, 0:1]
            zb_pair.append(jnp.sqrt(qn * kn) * SCORE_BOUND_SLACK + SCORE_BOUND_SLACK)
        for s in range(n_sub):
            qs[(c, s)] = q[s * sub:(s + 1) * sub, :]
            zbmax.extend(zb_pair)

    def scores(n, j):
        return lax.dot_general(qs[chains[n]], kk_ref[chains[n][0], j], (((1,), (1,)), ((), ())),
                               preferred_element_type=F32)

    def run(jobs, accs, zs=None, nxt=None):
        old = list(accs)
        new = list(accs)
        if zs is None:
            zs = [scores(n, j) for n, j, _ in jobs]
        sps = []
        for z, (n, _, mask) in zip(zs, jobs):
            sp = jnp.maximum(z, 0.0) + jnp.log2(1.0 + jnp.exp2(-jnp.abs(z)))
            sp = sp if mask is None else jnp.where(mask, sp, 0.0)
            sps.append(sp)
            new[2 * n] = old[2 * n] + jnp.sum(sp[:, :tk], axis=-1, keepdims=True)
            new[2 * n + 1] = old[2 * n + 1] + jnp.sum(sp[:, tk:], axis=-1, keepdims=True)
        incls = [jnp.dot(sp.astype(BF16), mbd_ref[...], preferred_element_type=F32)
                 for sp in sps]
        if nxt is not None:
            for n, _, _ in jobs:
                z_ref[n] = scores(n, nxt)
        ps = []
        for z, incl, (n, _, mask) in zip(zs, incls, jobs):
            la0 = z[:, :tk] - incl[:, :tk] - old[2 * n]
            la1 = z[:, tk:] - incl[:, tk:] - old[2 * n + 1]
            p = jnp.concatenate([jnp.exp2(la0), jnp.exp2(la1)], axis=1)
            ps.append((p if mask is None else jnp.where(mask, p, 0.0)).astype(BF16))
        for p, (n, j, _) in zip(ps, jobs):
            c, s = chains[n]
            acc_ref[s * sub:(s + 1) * sub, c * LANES_V7X:(c + 1) * LANES_V7X] += jnp.dot(
                p, vv_ref[c, j], preferred_element_type=F32)
        return new

    def live(accs):
        worst = zbmax[0] - jnp.min(accs[0], axis=0, keepdims=True)
        for zb, a in zip(zbmax[1:], accs[1:]):
            worst = jnp.maximum(worst, zb - jnp.min(a, axis=0, keepdims=True))
        return (jnp.max(worst) > F32_MIN_EXP2).astype(jnp.int32)

    accs = [jnp.zeros((sub, 1), F32) for _ in range(2 * len(chains))]
    j_diag = qi * n_diag
    for d in range(n_diag - 1, -1, -1):
        jobs = []
        for n, (c, s) in enumerate(chains):
            if d * tk >= (s + 1) * sub - 1:
                continue
            all_visible = (d + 1) * tk <= s * sub
            jobs.append((n, j_diag + d, None if all_visible else d * tk + s_off < s * sub + row))
        accs = run(jobs, accs, nxt=jnp.maximum(j_diag - 1, 0) if d == 0 else None)

    def cond(carry):
        return jnp.logical_and(carry[0] < j_diag, carry[1] > 0)

    def body(carry):
        j = j_diag - 1 - carry[0]
        zs = [z_ref[n] for n in range(len(chains))]
        accs = run([(n, j, None) for n in range(len(chains))], carry[2:], zs=zs,
                   nxt=jnp.maximum(j - 1, 0))
        return (carry[0] + 1, live(accs)) + tuple(accs)

    lax.while_loop(cond, body, (jnp.int32(0), live(accs)) + tuple(accs))
    o_ref[...] = acc_ref[...].astype(o_ref.dtype)


def _attention(qkv, batch, seq, d_model):
    tq = min(ATTN_Q_TILE, seq)
    tk = ATTN_K_TILE
    nq = seq // tq
    width = ATTN_PAIRS_PER_STEP * LANES_V7X
    n_groups = d_model // width
    r = jnp.arange(2 * tk)
    mbd = ((r[:, None] // tk == r[None, :] // tk) & (r[:, None] >= r[None, :])).astype(BF16)
    kern = functools.partial(_attn_kernel, tq=tq)
    return pl.pallas_call(
        kern,
        grid=(batch, n_groups, nq),
        in_specs=[pl.BlockSpec((tq, width), lambda b, p, i: (b * nq + i, p)),
                  pl.BlockSpec((seq, width), lambda b, p, i: (b, n_groups + p)),
                  pl.BlockSpec((seq, width), lambda b, p, i: (b, 2 * n_groups + p)),
                  pl.BlockSpec((2 * tk, 2 * tk), lambda b, p, i: (0, 0))],
        out_specs=pl.BlockSpec((tq, width), lambda b, p, i: (b * nq + i, p)),
        out_shape=jax.ShapeDtypeStruct((batch * seq, d_model), BF16),
        scratch_shapes=[pltpu.VMEM((ATTN_PAIRS_PER_STEP, seq // tk, 2 * tk, LANES_V7X), BF16),
                        pltpu.VMEM((ATTN_PAIRS_PER_STEP, seq // tk, 2 * tk, LANES_V7X), BF16),
                        pltpu.VMEM((tq, width), F32),
                        pltpu.VMEM((2 * ATTN_PAIRS_PER_STEP * SUBLANES_V7X, LANES_V7X), F32),
                        pltpu.VMEM((ATTN_PAIRS_PER_STEP * (tq // min(ATTN_SUB_ROWS, tq)),
                                    min(ATTN_SUB_ROWS, tq), 2 * tk), F32)],
        compiler_params=_params("parallel", "parallel", "arbitrary"),
        name="stickbreak_attn",
    )(qkv, qkv, qkv, mbd)


def _router_kernel(x_ref, w_ref, b_ref, ltri_ref, info_ref, cnt_ref, carry_ref):
    tm = x_ref.shape[0]

    @pl.when(pl.program_id(0) == 0)
    def _():
        carry_ref[...] = jnp.zeros_like(carry_ref)

    logits = jnp.dot(x_ref[...], w_ref[...], preferred_element_type=F32,
                     precision=lax.Precision.HIGHEST) + b_ref[...]
    lane = lax.broadcasted_iota(jnp.int32, (tm, LANES_V7X), 1).astype(F32)
    big = float(LANES_V7X)
    m1 = jnp.max(logits, axis=-1, keepdims=True)
    i1 = jnp.min(jnp.where(logits == m1, lane, big), axis=-1, keepdims=True)
    l2 = jnp.where(lane == i1, -jnp.inf, logits)
    m2 = jnp.max(l2, axis=-1, keepdims=True)
    i2 = jnp.min(jnp.where(l2 == m2, lane, big), axis=-1, keepdims=True)
    e = jnp.exp(m2 - m1)
    g1 = 1.0 / (1.0 + e)
    g2 = e * g1
    hit1 = lane == i1
    hit2 = lane == i2
    oh = jnp.where(jnp.logical_or(hit1, hit2), 1.0, 0.0)
    cum = jnp.dot(ltri_ref[...], oh.astype(BF16), preferred_element_type=F32)
    before = cum - oh + carry_ref[0:1, :]
    r1 = jnp.sum(jnp.where(hit1, before, 0.0), axis=-1, keepdims=True)
    r2 = jnp.sum(jnp.where(hit2, before, 0.0), axis=-1, keepdims=True)
    total = carry_ref[0:1, :] + cum[tm - 1:tm, :]
    carry_ref[...] = jnp.broadcast_to(total, carry_ref.shape)
    cnt_ref[...] = jnp.broadcast_to(total, cnt_ref.shape)
    info = jnp.where(lane == 0.0, i1, 0.0)
    info = jnp.where(lane == 1.0, i2, info)
    info = jnp.where(lane == 2.0, g1, info)
    info = jnp.where(lane == 3.0, g2, info)
    info = jnp.where(lane == 4.0, r1, info)
    info = jnp.where(lane == 5.0, r2, info)
    info_ref[...] = info


def _router(x, w_router, b_router):
    t, d = x.shape
    tm = min(ROUTE_TILE, t)
    pad = LANES_V7X - N_EXPERTS
    w = jnp.pad(w_router, ((0, 0), (0, pad)))
    b = jnp.concatenate([b_router.astype(F32), jnp.full((pad,), -1e30, F32)]).reshape(1, LANES_V7X)
    ltri = jnp.tril(jnp.ones((tm, tm), BF16))
    return pl.pallas_call(
        _router_kernel,
        grid=(t // tm,),
        in_specs=[pl.BlockSpec((tm, d), lambda i: (i, 0)),
                  pl.BlockSpec((d, LANES_V7X), lambda i: (0, 0)),
                  pl.BlockSpec((1, LANES_V7X), lambda i: (0, 0)),
                  pl.BlockSpec((tm, tm), lambda i: (0, 0))],
        out_specs=[pl.BlockSpec((tm, LANES_V7X), lambda i: (i, 0)),
                   pl.BlockSpec((SUBLANES_V7X, LANES_V7X), lambda i: (0, 0))],
        out_shape=[jax.ShapeDtypeStruct((t, LANES_V7X), F32),
                   jax.ShapeDtypeStruct((SUBLANES_V7X, LANES_V7X), F32)],
        scratch_shapes=[pltpu.VMEM((SUBLANES_V7X, LANES_V7X), F32)],
        compiler_params=_params("arbitrary"),
        name="moe_router",
    )(x, w, b, ltri)


def _row_copy(src_ref, src_row, dst_ref, dst_row, sem):
    return pltpu.make_async_copy(src_ref.at[pl.ds(src_row, 1)],
                                 dst_ref.at[pl.ds(dst_row, 1)], sem)


def _dispatch_kernel(d1_ref, d2_ref, x_ref, buf_in_hbm, buf_hbm, sem):
    del buf_in_hbm
    tm = d1_ref.shape[0]

    def issue(t, c):
        _row_copy(x_ref, t, buf_hbm, d1_ref[t], sem).start()
        _row_copy(x_ref, t, buf_hbm, d2_ref[t], sem).start()
        return c
    lax.fori_loop(0, tm, issue, 0, unroll=8)

    def drain(t, c):
        _row_copy(x_ref, 0, buf_hbm, 0, sem).wait()
        _row_copy(x_ref, 0, buf_hbm, 0, sem).wait()
        return c
    lax.fori_loop(0, tm, drain, 0, unroll=8)


def _dispatch(x, dest1, dest2, n_slots):
    t, d = x.shape
    tm = min(ROUTE_TILE, t)
    buf0 = jnp.zeros((n_slots, d), x.dtype)
    smem = lambda: pl.BlockSpec((tm,), lambda i: (i,), memory_space=pltpu.SMEM)
    return pl.pallas_call(
        _dispatch_kernel,
        grid=(t // tm,),
        in_specs=[smem(), smem(),
                  pl.BlockSpec((tm, d), lambda i: (i, 0)),
                  pl.BlockSpec(memory_space=pl.ANY)],
        out_specs=pl.BlockSpec(memory_space=pl.ANY),
        out_shape=jax.ShapeDtypeStruct((n_slots, d), x.dtype),
        scratch_shapes=[pltpu.SemaphoreType.DMA],
        input_output_aliases={3: 0},
        compiler_params=_params("arbitrary"),
        name="moe_dispatch",
    )(dest1, dest2, x, buf0)


def _combine_kernel(d1_ref, d2_ref, info_ref, y_hbm, res_ref, g_ref, b_ref, o_ref,
                    buf1_ref, buf2_ref, sem):
    tm = d1_ref.shape[0]

    def issue(t, c):
        _row_copy(y_hbm, d1_ref[t], buf1_ref, t, sem).start()
        _row_copy(y_hbm, d2_ref[t], buf2_ref, t, sem).start()
        return c
    lax.fori_loop(0, tm, issue, 0, unroll=8)

    def drain(t, c):
        _row_copy(y_hbm, 0, buf1_ref, 0, sem).wait()
        _row_copy(y_hbm, 0, buf2_ref, 0, sem).wait()
        return c
    lax.fori_loop(0, tm, drain, 0, unroll=8)

    g1 = info_ref[:, 2:3]
    g2 = info_ref[:, 3:4]
    y = g1 * buf1_ref[...] + g2 * buf2_ref[...]
    o_ref[...] = _layer_norm_rows(DEEPNORM_ALPHA * res_ref[...] + y, g_ref[...], b_ref[...])


def _combine_ln(y_buf, dest1, dest2, info, res, g, b):
    t, d = res.shape
    tm = min(ROUTE_TILE, t)
    smem = lambda: pl.BlockSpec((tm,), lambda i: (i,), memory_space=pltpu.SMEM)
    return pl.pallas_call(
        _combine_kernel,
        grid=(t // tm,),
        in_specs=[smem(), smem(),
                  pl.BlockSpec((tm, LANES_V7X), lambda i: (i, 0)),
                  pl.BlockSpec(memory_space=pl.ANY),
                  pl.BlockSpec((tm, d), lambda i: (i, 0)),
                  pl.BlockSpec((1, d), lambda i: (0, 0)),
                  pl.BlockSpec((1, d), lambda i: (0, 0))],
        out_specs=pl.BlockSpec((tm, d), lambda i: (i, 0)),
        out_shape=jax.ShapeDtypeStruct((t, d), F32),
        scratch_shapes=[pltpu.VMEM((tm, d), F32), pltpu.VMEM((tm, d), F32),
                        pltpu.SemaphoreType.DMA],
        compiler_params=_params("arbitrary"),
        name="moe_combine_ln",
    )(dest1, dest2, info, y_buf, res, g, b)


def _pad_cols(w, n):
    return jnp.pad(w, ((0, 0), (0, n - w.shape[1])))


def _mamba_layer(x, xb, batch, seq, ln_g, ln_b, w_in, conv_w, conv_b, dt_bias, a_log, d_skip,
                 norm_w, w_out, w_gate, w_up, w_down):
    d_model = x.shape[1]
    n_heads = a_log.shape[0]
    d_inner = n_heads * SSM_HEAD_DIM
    conv_dim = conv_w.shape[1]
    w_z, w_xbc, w_dt = jnp.split(w_in, [d_inner, d_inner + conv_dim], axis=1)
    w_main = jnp.concatenate([w_xbc, w_z], axis=1).astype(BF16)
    zx = _matmul_resident(xb, w_main, BF16, min(ROW_TILE, x.shape[0]), 2 * ROW_TILE,
                          "ssm_in_proj")
    dt_raw = _matmul(xb, _pad_cols(w_dt, LANES_V7X).astype(BF16), F32,
                     min(ROW_TILE * 2, x.shape[0]), LANES_V7X, "ssm_dt_proj")
    y = _ssd(zx, dt_raw, conv_w, conv_b, dt_bias, a_log, d_skip, norm_w, batch, seq, d_inner,
             n_heads)
    g0 = ln_g[0].reshape(1, d_model)
    b0 = ln_b[0].reshape(1, d_model)
    x1, x1b = _matmul_ln(y, w_out.astype(BF16), x, g0, b0, min(ROW_TILE, x.shape[0]),
                         "ssm_out_proj_ln")
    f = w_gate.shape[1]
    fp = -(-f // (2 * LANES_V7X)) * (2 * LANES_V7X)
    wg = _pad_cols(w_gate, fp).astype(BF16)[None]
    wu = _pad_cols(w_up, fp).astype(BF16)[None]
    wd = jnp.pad(w_down, ((0, fp - f), (0, 0))).astype(BF16)[None]
    g1 = ln_g[1].reshape(1, d_model)
    b1 = ln_b[1].reshape(1, d_model)
    return _ffn_ln(x1b, wg, wu, wd, x1, g1, b1, min(ROW_TILE, x.shape[0]), fp,
                   "dense_swiglu_ln")


def _attn_moe_layer(x, xb, batch, seq, ln_g, ln_b, w_qkv, w_o, w_router, b_router,
                    w_gate, w_up, w_down):
    t, d_model = x.shape
    qscale = jnp.concatenate([jnp.full((d_model,), LOG2E / math.sqrt(SB_HEAD_DIM), F32),
                              jnp.ones((2 * d_model,), F32)])
    w_qkv_b = (w_qkv * qscale).astype(BF16)
    qkv = _matmul_resident(xb, w_qkv_b, BF16, min(ROW_TILE * 2, t), d_model, "sb_qkv_proj")
    o = _attention(qkv, batch, seq, d_model)
    g0 = ln_g[0].reshape(1, d_model)
    b0 = ln_b[0].reshape(1, d_model)
    x1, _ = _matmul_ln(o, w_o.astype(BF16), x, g0, b0, min(ROW_TILE, t), "sb_out_proj_ln")

    info, cnt = _router(x1, w_router, b_router)
    counts = cnt[0, :N_EXPERTS].astype(jnp.int32)
    padded = (counts + MOE_ROWS - 1) // MOE_ROWS * MOE_ROWS
    pad_end = jnp.cumsum(padded)
    pad_start = pad_end - padded
    e1 = info[:, 0].astype(jnp.int32)
    e2 = info[:, 1].astype(jnp.int32)
    dest1 = pad_start[e1] + info[:, 4].astype(jnp.int32)
    dest2 = pad_start[e2] + info[:, 5].astype(jnp.int32)
    n_slots = 2 * t + N_EXPERTS * MOE_ROWS
    n_blocks = n_slots // MOE_ROWS
    block_start = jnp.arange(n_blocks) * MOE_ROWS
    block_expert = jnp.minimum(
        jnp.sum(pad_end[None, :] <= block_start[:, None], axis=1), N_EXPERTS - 1).astype(jnp.int32)
    n_used = (pad_end[-1] // MOE_ROWS).astype(jnp.int32).reshape(1)

    x_buf = _dispatch(x1, dest1, dest2, n_slots)
    f = w_gate.shape[2]
    y_buf = _ffn_experts(x_buf, block_expert, n_used, w_gate.astype(BF16), w_up.astype(BF16),
                         w_down.astype(BF16), MOE_ROWS, f // 2, "moe_expert_swiglu")
    g1 = ln_g[1].reshape(1, d_model)
    b1 = ln_b[1].reshape(1, d_model)
    return _combine_ln(y_buf, dest1, dest2, info, x1, g1, b1)


def kernel(x, ln_g, ln_b, ssm_w_in, ssm_conv_w, ssm_conv_b, ssm_dt_bias, ssm_a_log, ssm_d,
           ssm_norm_w, ssm_w_out, sb_w_qkv, sb_w_o, ffn_w_gate, ffn_w_up, ffn_w_down,
           moe_w_router, moe_b_router, moe_w_gate, moe_w_up, moe_w_down):
    batch, seq, d_model = x.shape
    xf = x.reshape(batch * seq, d_model)
    x1, x1b = _mamba_layer(xf, xf.astype(BF16), batch, seq, ln_g[0], ln_b[0], ssm_w_in[0],
                           ssm_conv_w[0], ssm_conv_b[0], ssm_dt_bias[0], ssm_a_log[0], ssm_d[0],
                           ssm_norm_w[0], ssm_w_out[0], ffn_w_gate[0], ffn_w_up[0], ffn_w_down[0])
    out = _attn_moe_layer(x1, x1b, batch, seq, ln_g[1], ln_b[1], sb_w_qkv[0], sb_w_o[0],
                          moe_w_router[0], moe_b_router[0], moe_w_gate[0], moe_w_up[0],
                          moe_w_down[0])
    return out.reshape(batch, seq, d_model)
```

```python
import functools
import math

import jax
import jax.numpy as jnp
from jax import lax
from jax.experimental import pallas as pl
from jax.experimental.pallas import tpu as pltpu

F32 = jnp.float32
BF16 = jnp.bfloat16

LANES_V7X = 128
SUBLANES_V7X = 8
VMEM_LIMIT_BYTES_V7X = 56 * 1024 * 1024

DEPTH = 2
SSM_HEAD_DIM = 64
SSM_GROUPS = 8
SSM_D_STATE = 128
SSM_CONV = 4
SSM_CHUNK = 128
SB_HEAD_DIM = 64
N_EXPERTS = 8
DEEPNORM_ALPHA = (2.0 * DEPTH) ** 0.25
LN_EPS = 1e-5
RMS_EPS = 1e-5
LOG2E = 1.4426950408889634
LN2 = 0.6931471805599453
F32_MIN_EXP2 = -150.0
SCORE_BOUND_SLACK = 1.01

ROW_TILE = 512
MOE_ROWS = 512
ATTN_Q_TILE = 256
ATTN_K_TILE = 128
ATTN_PAIRS_PER_STEP = 2
ATTN_SUB_ROWS = 128
ROUTE_TILE = 1024


def _params(*semantics):
    return pltpu.CompilerParams(dimension_semantics=semantics,
                                vmem_limit_bytes=VMEM_LIMIT_BYTES_V7X)


def _layer_norm_rows(y, g, b):
    mu = jnp.mean(y, axis=-1, keepdims=True)
    d = y - mu
    var = jnp.mean(d * d, axis=-1, keepdims=True)
    return d * lax.rsqrt(var + LN_EPS) * g + b


def _silu(x):
    h = 0.5 * x
    return h * jnp.tanh(h) + h


def _softplus(x):
    return jnp.maximum(x, 0.0) + jnp.log(1.0 + jnp.exp(-jnp.abs(x)))


def _bdot(a, b):
    return jnp.dot(a.astype(BF16), b.astype(BF16), preferred_element_type=F32)


def _mm_kernel(x_ref, w_ref, o_ref):
    o_ref[...] = jnp.dot(x_ref[...], w_ref[...],
                         preferred_element_type=F32).astype(o_ref.dtype)


def _matmul(x, w, out_dtype, tm, tn, name):
    m, k = x.shape
    n = w.shape[1]
    return pl.pallas_call(
        _mm_kernel,
        grid=(m // tm, n // tn),
        in_specs=[pl.BlockSpec((tm, k), lambda i, j: (i, 0)),
                  pl.BlockSpec((k, tn), lambda i, j: (0, j))],
        out_specs=pl.BlockSpec((tm, tn), lambda i, j: (i, j)),
        out_shape=jax.ShapeDtypeStruct((m, n), out_dtype),
        compiler_params=_params("parallel", "arbitrary"),
        name=name,
    )(x, w)


def _mm_resident_kernel(x_ref, w_ref, o_ref, *, chunk, silu_from):
    x = x_ref[...]
    for c in range(o_ref.shape[1] // chunk):
        cols = slice(c * chunk, (c + 1) * chunk)
        r = jnp.dot(x, w_ref[:, cols], preferred_element_type=F32)
        if silu_from is not None and c * chunk >= silu_from:
            r = _silu(r)
        o_ref[:, cols] = r.astype(o_ref.dtype)


def _matmul_resident(x, w, out_dtype, tm, chunk, name, silu_from=None):
    m, k = x.shape
    n = w.shape[1]
    return pl.pallas_call(
        functools.partial(_mm_resident_kernel, chunk=chunk, silu_from=silu_from),
        grid=(m // tm,),
        in_specs=[pl.BlockSpec((tm, k), lambda i: (i, 0)),
                  pl.BlockSpec((k, n), lambda i: (0, 0), pipeline_mode=pl.Buffered(1))],
        out_specs=pl.BlockSpec((tm, n), lambda i: (i, 0)),
        out_shape=jax.ShapeDtypeStruct((m, n), out_dtype),
        compiler_params=_params("parallel"),
        name=name,
    )(x, w)


def _mm_ln_kernel(a_ref, w_ref, res_ref, g_ref, b_ref, o_ref, ob_ref):
    h = jnp.dot(a_ref[...], w_ref[...], preferred_element_type=F32)
    o = _layer_norm_rows(DEEPNORM_ALPHA * res_ref[...] + h, g_ref[...], b_ref[...])
    o_ref[...] = o
    ob_ref[...] = o.astype(BF16)


def _matmul_ln(a, w, res, g, b, tm, name):
    m, k = a.shape
    n = w.shape[1]
    return pl.pallas_call(
        _mm_ln_kernel,
        grid=(m // tm,),
        in_specs=[pl.BlockSpec((tm, k), lambda i: (i, 0)),
                  pl.BlockSpec((k, n), lambda i: (0, 0)),
                  pl.BlockSpec((tm, n), lambda i: (i, 0)),
                  pl.BlockSpec((1, n), lambda i: (0, 0)),
                  pl.BlockSpec((1, n), lambda i: (0, 0))],
        out_specs=[pl.BlockSpec((tm, n), lambda i: (i, 0)),
                   pl.BlockSpec((tm, n), lambda i: (i, 0))],
        out_shape=[jax.ShapeDtypeStruct((m, n), F32),
                   jax.ShapeDtypeStruct((m, n), BF16)],
        compiler_params=_params("parallel"),
        name=name,
    )(a, w, res, g, b)


def _ffn_body(eid_ref, nused_ref, x_ref, wg_ref, wu_ref, wd_ref, acc_ref):
    j = pl.program_id(1)
    x = x_ref[...].astype(BF16)
    g = jnp.dot(x, wg_ref[0], preferred_element_type=F32)
    u = jnp.dot(x, wu_ref[0], preferred_element_type=F32)
    h = (_silu(g) * u).astype(BF16)
    part = jnp.dot(h, wd_ref[0], preferred_element_type=F32)

    @pl.when(j == 0)
    def _():
        acc_ref[...] = part

    @pl.when(j > 0)
    def _():
        acc_ref[...] += part


def _ffn_ln_kernel(eid_ref, nused_ref, x_ref, wg_ref, wu_ref, wd_ref, res_ref, g_ref, b_ref,
                   o_ref, ob_ref, acc_ref):
    _ffn_body(eid_ref, nused_ref, x_ref, wg_ref, wu_ref, wd_ref, acc_ref)

    @pl.when(pl.program_id(1) == pl.num_programs(1) - 1)
    def _():
        o = _layer_norm_rows(DEEPNORM_ALPHA * res_ref[...] + acc_ref[...],
                             g_ref[...], b_ref[...])
        o_ref[...] = o
        ob_ref[...] = o.astype(BF16)


def _ffn_raw_kernel(eid_ref, nused_ref, x_ref, wg_ref, wu_ref, wd_ref, o_ref, acc_ref):
    i = pl.program_id(0)
    last = pl.program_id(1) == pl.num_programs(1) - 1
    used = i < nused_ref[0]

    @pl.when(used)
    def _():
        _ffn_body(eid_ref, nused_ref, x_ref, wg_ref, wu_ref, wd_ref, acc_ref)

    @pl.when(jnp.logical_and(used, last))
    def _():
        o_ref[...] = acc_ref[...]

    @pl.when(jnp.logical_and(jnp.logical_not(used), last))
    def _():
        o_ref[...] = jnp.zeros_like(o_ref)


def _ffn_specs(tm, d, tf, nf, fixed_weights=False):
    def wcol(i, j, eid, nused):
        return (eid[i], 0, jnp.where(i < nused[0], j, nf - 1))

    def wrow(i, j, eid, nused):
        return (eid[i], jnp.where(i < nused[0], j, nf - 1), 0)

    mode = dict(pipeline_mode=pl.Buffered(1)) if fixed_weights else {}
    return [pl.BlockSpec((tm, d), lambda i, j, eid, nused: (i, 0)),
            pl.BlockSpec((1, d, tf), wcol, **mode),
            pl.BlockSpec((1, d, tf), wcol, **mode),
            pl.BlockSpec((1, tf, d), wrow, **mode)]


def _ffn_ln(xb, wg, wu, wd, res, g, b, tm, tf, name):
    m, d = xb.shape
    f = wg.shape[2]
    nf = f // tf
    nblk = m // tm
    eid = jnp.zeros((nblk,), jnp.int32)
    nused = jnp.full((1,), nblk, jnp.int32)
    row = lambda i, j, eid, nused: (i, 0)
    vec = lambda i, j, eid, nused: (0, 0)
    return pl.pallas_call(
        _ffn_ln_kernel,
        grid_spec=pltpu.PrefetchScalarGridSpec(
            num_scalar_prefetch=2,
            grid=(nblk, nf),
            in_specs=_ffn_specs(tm, d, tf, nf, fixed_weights=(nf == 1)) + [
                pl.BlockSpec((tm, d), row),
                pl.BlockSpec((1, d), vec),
                pl.BlockSpec((1, d), vec)],
            out_specs=[pl.BlockSpec((tm, d), row), pl.BlockSpec((tm, d), row)],
            scratch_shapes=[pltpu.VMEM((tm, d), F32)]),
        out_shape=[jax.ShapeDtypeStruct((m, d), F32),
                   jax.ShapeDtypeStruct((m, d), BF16)],
        compiler_params=_params("parallel", "arbitrary"),
        name=name,
    )(eid, nused, xb, wg, wu, wd, res, g, b)


def _ffn_experts(x_buf, eid, nused, wg, wu, wd, tm, tf, name):
    m, d = x_buf.shape
    f = wg.shape[2]
    nf = f // tf
    nblk = m // tm
    row = lambda i, j, eid, nused: (i, 0)
    return pl.pallas_call(
        _ffn_raw_kernel,
        grid_spec=pltpu.PrefetchScalarGridSpec(
            num_scalar_prefetch=2,
            grid=(nblk, nf),
            in_specs=_ffn_specs(tm, d, tf, nf),
            out_specs=pl.BlockSpec((tm, d), row),
            scratch_shapes=[pltpu.VMEM((tm, d), F32)]),
        out_shape=jax.ShapeDtypeStruct((m, d), F32),
        compiler_params=_params("parallel", "arbitrary"),
        name=name,
    )(eid, nused, x_buf, wg, wu, wd)


def _split2(v):
    hi = v.astype(BF16)
    lo = (v - hi.astype(F32)).astype(BF16)
    return hi, lo


def _expand_cols(parts, e):
    out = jnp.dot(parts[0], e, preferred_element_type=F32)
    for p in parts[1:]:
        out += jnp.dot(p, e, preferred_element_type=F32)
    return out


def _ssd_kernel(xbc_ref, gate_ref, dt_ref, convw_ref, convb_ref, dtb_ref, alog_ref, dexp_ref,
                normw_ref, ltri_ref, sel128_ref, sel64_ref, y_ref,
                ext_ref, act_ref, eoff_ref, state_ref, *, d_inner, n_heads):
    L = SSM_CHUNK
    N = SSM_D_STATE
    G = SSM_GROUPS
    hpg = n_heads // G
    gw = d_inner // G
    halo = SUBLANES_V7X

    @pl.when(pl.program_id(1) == 0)
    def _():
        ext_ref[0:halo, :] = jnp.zeros((halo, ext_ref.shape[1]), F32)
        state_ref[...] = jnp.zeros_like(state_ref)

    ext_ref[halo:halo + L, :] = xbc_ref[...].astype(F32)
    conv = convb_ref[...]
    for k in range(SSM_CONV):
        conv = conv + convw_ref[k:k + 1, :] * ext_ref[pl.ds(halo - (SSM_CONV - 1) + k, L), :]
    ext_ref[0:halo, :] = ext_ref[L:L + halo, :]
    act_ref[...] = _silu(conv)

    dt = _softplus(dt_ref[...] + dtb_ref[...])
    a = -jnp.exp(alog_ref[...])
    acum = jnp.dot(ltri_ref[...], dt * a, preferred_element_type=F32,
                   precision=lax.Precision.HIGHEST)
    acum_last = acum[L - 1:L, :]
    acum_t = acum.T
    dt_t = dt.T
    acum_parts = _split2(acum)
    eoff_ref[...] = _expand_cols(_split2(jnp.exp(acum)), sel64_ref[...])
    eend = _expand_cols(_split2(jnp.exp(acum_last - acum) * dt), sel64_ref[...])

    row = lax.broadcasted_iota(jnp.int32, (L, L), 0)
    col = lax.broadcasted_iota(jnp.int32, (L, L), 1)
    causal = col <= row
    glane = lax.broadcasted_iota(jnp.int32, (L, gw), 1) // SSM_HEAD_DIM

    for g in range(G):
        xs_g = act_ref[:, g * gw:(g + 1) * gw]
        xs_b = xs_g.astype(BF16)
        b_g = act_ref[:, d_inner + g * N:d_inner + (g + 1) * N]
        c_gb = act_ref[:, d_inner + G * N + g * N:d_inner + G * N + (g + 1) * N].astype(BF16)
        b_gt = b_g.T.astype(BF16)
        cb = jnp.dot(c_gb, b_gt, preferred_element_type=F32)
        acol = _expand_cols(acum_parts,
                            sel128_ref[:, g * hpg * L:(g + 1) * hpg * L])
        y_g = jnp.zeros((L, gw), F32)
        for h in range(hpg):
            hh = g * hpg + h
            seg = acol[:, h * L:(h + 1) * L] - acum_t[hh:hh + 1, :]
            decay = jnp.exp(jnp.where(causal, seg, -jnp.inf))
            w = cb * decay * dt_t[hh:hh + 1, :]
            xs_h = jnp.where(glane == h, xs_b, jnp.zeros_like(xs_b))
            y_g = y_g + jnp.dot(w.astype(BF16), xs_h, preferred_element_type=F32)
        state_g = state_ref[:, g * gw:(g + 1) * gw]
        eoff_g = eoff_ref[:, g * gw:(g + 1) * gw]
        y_g = y_g + _bdot(c_gb, state_g) * eoff_g
        xs_scaled = xs_g * eend[:, g * gw:(g + 1) * gw]
        state_ref[:, g * gw:(g + 1) * gw] = (
            state_g * eoff_g[L - 1:L, :]
            + jnp.dot(b_gt, xs_scaled.astype(BF16), preferred_element_type=F32))
        y_g = y_g + dexp_ref[:, g * gw:(g + 1) * gw] * xs_g
        y_g = y_g * gate_ref[:, g * gw:(g + 1) * gw].astype(F32)
        ms = jnp.mean(y_g * y_g, axis=-1, keepdims=True)
        y_g = y_g * lax.rsqrt(ms + RMS_EPS) * normw_ref[:, g * gw:(g + 1) * gw]
        y_ref[:, g * gw:(g + 1) * gw] = y_g.astype(y_ref.dtype)


def _ssd(zx, dt_raw, conv_w, conv_b, dt_bias, a_log, d_skip, norm_w, batch, seq, d_inner,
         n_heads):
    L = SSM_CHUNK
    conv_dim = d_inner + 2 * SSM_GROUPS * SSM_D_STATE
    nc = seq // L
    pad = LANES_V7X - n_heads
    dtb = jnp.pad(dt_bias, (0, pad)).reshape(1, LANES_V7X)
    alog = jnp.pad(a_log, (0, pad)).reshape(1, LANES_V7X)
    dexp = jnp.repeat(d_skip, SSM_HEAD_DIM).reshape(1, d_inner)
    ltri = jnp.tril(jnp.ones((L, L), F32))
    hidx = jnp.arange(LANES_V7X)[:, None]
    sel128 = (hidx == (jnp.arange(n_heads * L) // L)[None, :]).astype(BF16)
    sel64 = (hidx == (jnp.arange(d_inner) // SSM_HEAD_DIM)[None, :]).astype(BF16)
    z_blk = conv_dim // d_inner
    const = lambda b, c: (0, 0)
    kern = functools.partial(_ssd_kernel, d_inner=d_inner, n_heads=n_heads)
    return pl.pallas_call(
        kern,
        grid=(batch, nc),
        in_specs=[pl.BlockSpec((L, conv_dim), lambda b, c: (b * nc + c, 0)),
                  pl.BlockSpec((L, d_inner), lambda b, c: (b * nc + c, z_blk)),
                  pl.BlockSpec((L, LANES_V7X), lambda b, c: (b * nc + c, 0)),
                  pl.BlockSpec((SSM_CONV, conv_dim), const),
                  pl.BlockSpec((1, conv_dim), const),
                  pl.BlockSpec((1, LANES_V7X), const),
                  pl.BlockSpec((1, LANES_V7X), const),
                  pl.BlockSpec((1, d_inner), const),
                  pl.BlockSpec((1, d_inner), const),
                  pl.BlockSpec((L, L), const),
                  pl.BlockSpec((LANES_V7X, n_heads * L), const),
                  pl.BlockSpec((LANES_V7X, d_inner), const)],
        out_specs=pl.BlockSpec((L, d_inner), lambda b, c: (b * nc + c, 0)),
        out_shape=jax.ShapeDtypeStruct((batch * seq, d_inner), BF16),
        scratch_shapes=[pltpu.VMEM((L + SUBLANES_V7X, conv_dim), F32),
                        pltpu.VMEM((L, conv_dim), F32),
                        pltpu.VMEM((L, d_inner), F32),
                        pltpu.VMEM((SSM_D_STATE, d_inner), F32)],
        compiler_params=_params("parallel", "arbitrary"),
        name="ssd_scan",
    )(zx, zx, dt_raw, conv_w, conv_b.reshape(1, conv_dim), dtb, alog, dexp,
      norm_w.reshape(1, d_inner), ltri, sel128, sel64)


def _attn_kernel(q_ref, k_ref, v_ref, mbd_ref, o_ref, kk_ref, vv_ref, acc_ref, kn_ref, z_ref,
                 *, tq):
    tk = ATTN_K_TILE
    qi = pl.program_id(2)
    n_kb = k_ref.shape[0] // tk
    n_diag = tq // tk
    lane = lax.broadcasted_iota(jnp.int32, (tk, LANES_V7X), 1)
    head0 = lane < SB_HEAD_DIM

    n_pairs = q_ref.shape[1] // LANES_V7X
    rows = SUBLANES_V7X

    @pl.when(qi == 0)
    def _():
        for c in range(n_pairs):
            cols = slice(c * LANES_V7X, (c + 1) * LANES_V7X)

            def fill(j, carry, c=c, cols=cols):
                m0, m1 = carry
                kb = k_ref[pl.ds(pl.multiple_of(j * tk, tk), tk), cols]
                vb = v_ref[pl.ds(pl.multiple_of(j * tk, tk), tk), cols]
                zero = jnp.zeros_like(kb)
                kk_ref[c, j, 0:tk, :] = jnp.where(head0, kb, zero)
                kk_ref[c, j, tk:2 * tk, :] = jnp.where(head0, zero, kb)
                vv_ref[c, j, 0:tk, :] = jnp.where(head0, vb, zero)
                vv_ref[c, j, tk:2 * tk, :] = jnp.where(head0, zero, vb)
                ksq = kb.astype(F32) * kb.astype(F32)
                n0 = jnp.sum(jnp.where(head0, ksq, 0.0), axis=-1, keepdims=True)
                n1 = jnp.sum(jnp.where(head0, 0.0, ksq), axis=-1, keepdims=True)
                return jnp.maximum(m0, n0), jnp.maximum(m1, n1)
            init = (jnp.zeros((tk, 1), F32), jnp.zeros((tk, 1), F32))
            m0, m1 = lax.fori_loop(0, n_kb, fill, init, unroll=4)
            kn_ref[(2 * c) * rows:(2 * c + 1) * rows, :] = jnp.broadcast_to(
                jnp.max(m0, axis=0, keepdims=True), (rows, LANES_V7X))
            kn_ref[(2 * c + 1) * rows:(2 * c + 2) * rows, :] = jnp.broadcast_to(
                jnp.max(m1, axis=0, keepdims=True), (rows, LANES_V7X))

    acc_ref[...] = jnp.zeros_like(acc_ref)
    sub = min(ATTN_SUB_ROWS, tq)
    n_sub = tq // sub
    s_off = lax.broadcasted_iota(jnp.int32, (sub, 2 * tk), 1) % tk
    row = lax.broadcasted_iota(jnp.int32, (sub, 2 * tk), 0)
    qhead0 = lax.broadcasted_iota(jnp.int32, (tq, LANES_V7X), 1) < SB_HEAD_DIM

    chains = [(c, s) for c in range(n_pairs) for s in range(n_sub)]
    qs = {}
    zbmax = []
    for c in range(n_pairs):
        q = q_ref[:, c * LANES_V7X:(c + 1) * LANES_V7X]
        qsq = q.astype(F32) * q.astype(F32)
        zb_pair = []
        for h, sel in enumerate((qhead0, jnp.logical_not(qhead0))):
            qn = jnp.max(jnp.sum(jnp.where(sel, qsq, 0.0), axis=-1, keepdims=True),
                         axis=0, keepdims=True)
            kn = kn_ref[(2 * c + h) * rows:(2 * c + h) * rows + 1, 0:1]
            zb_pair.append(jnp.sqrt(qn * kn) * SCORE_BOUND_SLACK + SCORE_BOUND_SLACK)
        for s in range(n_sub):
            qs[(c, s)] = q[s * sub:(s + 1) * sub, :]
            zbmax.extend(zb_pair)

    def scores(n, j):
        return lax.dot_general(qs[chains[n]], kk_ref[chains[n][0], j], (((1,), (1,)), ((), ())),
                               preferred_element_type=F32)

    def run(jobs, accs, zs=None, nxt=None):
        old = list(accs)
        new = list(accs)
        if zs is None:
            zs = [scores(n, j) for n, j, _ in jobs]
        sps = []
        for z, (n, _, mask) in zip(zs, jobs):
            sp = jnp.maximum(z, 0.0) + jnp.log2(1.0 + jnp.exp2(-jnp.abs(z)))
            sp = sp if mask is None else jnp.where(mask, sp, 0.0)
            sps.append(sp)
            new[2 * n] = old[2 * n] + jnp.sum(sp[:, :tk], axis=-1, keepdims=True)
            new[2 * n + 1] = old[2 * n + 1] + jnp.sum(sp[:, tk:], axis=-1, keepdims=True)
        incls = [jnp.dot(sp.astype(BF16), mbd_ref[...], preferred_element_type=F32)
                 for sp in sps]
        if nxt is not None:
            for n, _, _ in jobs:
                z_ref[n] = scores(n, nxt)
        ps = []
        for z, incl, (n, _, mask) in zip(zs, incls, jobs):
            la0 = z[:, :tk] - incl[:, :tk] - old[2 * n]
            la1 = z[:, tk:] - incl[:, tk:] - old[2 * n + 1]
            p = jnp.concatenate([jnp.exp2(la0), jnp.exp2(la1)], axis=1)
            ps.append((p if mask is None else jnp.where(mask, p, 0.0)).astype(BF16))
        for p, (n, j, _) in zip(ps, jobs):
            c, s = chains[n]
            acc_ref[s * sub:(s + 1) * sub, c * LANES_V7X:(c + 1) * LANES_V7X] += jnp.dot(
                p, vv_ref[c, j], preferred_element_type=F32)
        return new

    def live(accs):
        worst = zbmax[0] - jnp.min(accs[0], axis=0, keepdims=True)
        for zb, a in zip(zbmax[1:], accs[1:]):
            worst = jnp.maximum(worst, zb - jnp.min(a, axis=0, keepdims=True))
        return (jnp.max(worst) > F32_MIN_EXP2).astype(jnp.int32)

    accs = [jnp.zeros((sub, 1), F32) for _ in range(2 * len(chains))]
    j_diag = qi * n_diag
    for d in range(n_diag - 1, -1, -1):
        jobs = []
        for n, (c, s) in enumerate(chains):
            if d * tk >= (s + 1) * sub - 1:
                continue
            all_visible = (d + 1) * tk <= s * sub
            jobs.append((n, j_diag + d, None if all_visible else d * tk + s_off < s * sub + row))
        accs = run(jobs, accs, nxt=jnp.maximum(j_diag - 1, 0) if d == 0 else None)

    def cond(carry):
        return jnp.logical_and(carry[0] < j_diag, carry[1] > 0)

    def body(carry):
        j = j_diag - 1 - carry[0]
        zs = [z_ref[n] for n in range(len(chains))]
        accs = run([(n, j, None) for n in range(len(chains))], carry[2:], zs=zs,
                   nxt=jnp.maximum(j - 1, 0))
        return (carry[0] + 1, live(accs)) + tuple(accs)

    lax.while_loop(cond, body, (jnp.int32(0), live(accs)) + tuple(accs))
    o_ref[...] = acc_ref[...].astype(o_ref.dtype)


def _attention(qkv, batch, seq, d_model):
    tq = min(ATTN_Q_TILE, seq)
    tk = ATTN_K_TILE
    nq = seq // tq
    width = ATTN_PAIRS_PER_STEP * LANES_V7X
    n_groups = d_model // width
    r = jnp.arange(2 * tk)
    mbd = ((r[:, None] // tk == r[None, :] // tk) & (r[:, None] >= r[None, :])).astype(BF16)
    kern = functools.partial(_attn_kernel, tq=tq)
    return pl.pallas_call(
        kern,
        grid=(batch, n_groups, nq),
        in_specs=[pl.BlockSpec((tq, width), lambda b, p, i: (b * nq + i, p)),
                  pl.BlockSpec((seq, width), lambda b, p, i: (b, n_groups + p)),
                  pl.BlockSpec((seq, width), lambda b, p, i: (b, 2 * n_groups + p)),
                  pl.BlockSpec((2 * tk, 2 * tk), lambda b, p, i: (0, 0))],
        out_specs=pl.BlockSpec((tq, width), lambda b, p, i: (b * nq + i, p)),
        out_shape=jax.ShapeDtypeStruct((batch * seq, d_model), BF16),
        scratch_shapes=[pltpu.VMEM((ATTN_PAIRS_PER_STEP, seq // tk, 2 * tk, LANES_V7X), BF16),
                        pltpu.VMEM((ATTN_PAIRS_PER_STEP, seq // tk, 2 * tk, LANES_V7X), BF16),
                        pltpu.VMEM((tq, width), F32),
                        pltpu.VMEM((2 * ATTN_PAIRS_PER_STEP * SUBLANES_V7X, LANES_V7X), F32),
                        pltpu.VMEM((ATTN_PAIRS_PER_STEP * (tq // min(ATTN_SUB_ROWS, tq)),
                                    min(ATTN_SUB_ROWS, tq), 2 * tk), F32)],
        compiler_params=_params("parallel", "parallel", "arbitrary"),
        name="stickbreak_attn",
    )(qkv, qkv, qkv, mbd)


def _router_kernel(x_ref, w_ref, b_ref, ltri_ref, info_ref, cnt_ref, carry_ref):
    tm = x_ref.shape[0]

    @pl.when(pl.program_id(0) == 0)
    def _():
        carry_ref[...] = jnp.zeros_like(carry_ref)

    logits = jnp.dot(x_ref[...], w_ref[...], preferred_element_type=F32,
                     precision=lax.Precision.HIGHEST) + b_ref[...]
    lane = lax.broadcasted_iota(jnp.int32, (tm, LANES_V7X), 1).astype(F32)
    big = float(LANES_V7X)
    m1 = jnp.max(logits, axis=-1, keepdims=True)
    i1 = jnp.min(jnp.where(logits == m1, lane, big), axis=-1, keepdims=True)
    l2 = jnp.where(lane == i1, -jnp.inf, logits)
    m2 = jnp.max(l2, axis=-1, keepdims=True)
    i2 = jnp.min(jnp.where(l2 == m2, lane, big), axis=-1, keepdims=True)
    e = jnp.exp(m2 - m1)
    g1 = 1.0 / (1.0 + e)
    g2 = e * g1
    hit1 = lane == i1
    hit2 = lane == i2
    oh = jnp.where(jnp.logical_or(hit1, hit2), 1.0, 0.0)
    cum = jnp.dot(ltri_ref[...], oh.astype(BF16), preferred_element_type=F32)
    before = cum - oh + carry_ref[0:1, :]
    r1 = jnp.sum(jnp.where(hit1, before, 0.0), axis=-1, keepdims=True)
    r2 = jnp.sum(jnp.where(hit2, before, 0.0), axis=-1, keepdims=True)
    total = carry_ref[0:1, :] + cum[tm - 1:tm, :]
    carry_ref[...] = jnp.broadcast_to(total, carry_ref.shape)
    cnt_ref[...] = jnp.broadcast_to(total, cnt_ref.shape)
    info = jnp.where(lane == 0.0, i1, 0.0)
    info = jnp.where(lane == 1.0, i2, info)
    info = jnp.where(lane == 2.0, g1, info)
    info = jnp.where(lane == 3.0, g2, info)
    info = jnp.where(lane == 4.0, r1, info)
    info = jnp.where(lane == 5.0, r2, info)
    info_ref[...] = info


def _router(x, w_router, b_router):
    t, d = x.shape
    tm = min(ROUTE_TILE, t)
    pad = LANES_V7X - N_EXPERTS
    w = jnp.pad(w_router, ((0, 0), (0, pad)))
    b = jnp.concatenate([b_router.astype(F32), jnp.full((pad,), -1e30, F32)]).reshape(1, LANES_V7X)
    ltri = jnp.tril(jnp.ones((tm, tm), BF16))
    return pl.pallas_call(
        _router_kernel,
        grid=(t // tm,),
        in_specs=[pl.BlockSpec((tm, d), lambda i: (i, 0)),
                  pl.BlockSpec((d, LANES_V7X), lambda i: (0, 0)),
                  pl.BlockSpec((1, LANES_V7X), lambda i: (0, 0)),
                  pl.BlockSpec((tm, tm), lambda i: (0, 0))],
        out_specs=[pl.BlockSpec((tm, LANES_V7X), lambda i: (i, 0)),
                   pl.BlockSpec((SUBLANES_V7X, LANES_V7X), lambda i: (0, 0))],
        out_shape=[jax.ShapeDtypeStruct((t, LANES_V7X), F32),
                   jax.ShapeDtypeStruct((SUBLANES_V7X, LANES_V7X), F32)],
        scratch_shapes=[pltpu.VMEM((SUBLANES_V7X, LANES_V7X), F32)],
        compiler_params=_params("arbitrary"),
        name="moe_router",
    )(x, w, b, ltri)


def _row_copy(src_ref, src_row, dst_ref, dst_row, sem):
    return pltpu.make_async_copy(src_ref.at[pl.ds(src_row, 1)],
                                 dst_ref.at[pl.ds(dst_row, 1)], sem)


def _dispatch_kernel(d1_ref, d2_ref, x_ref, buf_in_hbm, buf_hbm, sem):
    del buf_in_hbm
    tm = d1_ref.shape[0]

    def issue(t, c):
        _row_copy(x_ref, t, buf_hbm, d1_ref[t], sem).start()
        _row_copy(x_ref, t, buf_hbm, d2_ref[t], sem).start()
        return c
    lax.fori_loop(0, tm, issue, 0, unroll=8)

    def drain(t, c):
        _row_copy(x_ref, 0, buf_hbm, 0, sem).wait()
        _row_copy(x_ref, 0, buf_hbm, 0, sem).wait()
        return c
    lax.fori_loop(0, tm, drain, 0, unroll=8)


def _dispatch(x, dest1, dest2, n_slots):
    t, d = x.shape
    tm = min(ROUTE_TILE, t)
    buf0 = jnp.zeros((n_slots, d), x.dtype)
    smem = lambda: pl.BlockSpec((tm,), lambda i: (i,), memory_space=pltpu.SMEM)
    return pl.pallas_call(
        _dispatch_kernel,
        grid=(t // tm,),
        in_specs=[smem(), smem(),
                  pl.BlockSpec((tm, d), lambda i: (i, 0)),
                  pl.BlockSpec(memory_space=pl.ANY)],
        out_specs=pl.BlockSpec(memory_space=pl.ANY),
        out_shape=jax.ShapeDtypeStruct((n_slots, d), x.dtype),
        scratch_shapes=[pltpu.SemaphoreType.DMA],
        input_output_aliases={3: 0},
        compiler_params=_params("arbitrary"),
        name="moe_dispatch",
    )(dest1, dest2, x, buf0)


def _combine_kernel(d1_ref, d2_ref, info_ref, y_hbm, res_ref, g_ref, b_ref, o_ref,
                    buf1_ref, buf2_ref, sem):
    tm = d1_ref.shape[0]

    def issue(t, c):
        _row_copy(y_hbm, d1_ref[t], buf1_ref, t, sem).start()
        _row_copy(y_hbm, d2_ref[t], buf2_ref, t, sem).start()
        return c
    lax.fori_loop(0, tm, issue, 0, unroll=8)

    def drain(t, c):
        _row_copy(y_hbm, 0, buf1_ref, 0, sem).wait()
        _row_copy(y_hbm, 0, buf2_ref, 0, sem).wait()
        return c
    lax.fori_loop(0, tm, drain, 0, unroll=8)

    g1 = info_ref[:, 2:3]
    g2 = info_ref[:, 3:4]
    y = g1 * buf1_ref[...] + g2 * buf2_ref[...]
    o_ref[...] = _layer_norm_rows(DEEPNORM_ALPHA * res_ref[...] + y, g_ref[...], b_ref[...])


def _combine_ln(y_buf, dest1, dest2, info, res, g, b):
    t, d = res.shape
    tm = min(ROUTE_TILE, t)
    smem = lambda: pl.BlockSpec((tm,), lambda i: (i,), memory_space=pltpu.SMEM)
    return pl.pallas_call(
        _combine_kernel,
        grid=(t // tm,),
        in_specs=[smem(), smem(),
                  pl.BlockSpec((tm, LANES_V7X), lambda i: (i, 0)),
                  pl.BlockSpec(memory_space=pl.ANY),
                  pl.BlockSpec((tm, d), lambda i: (i, 0)),
                  pl.BlockSpec((1, d), lambda i: (0, 0)),
                  pl.BlockSpec((1, d), lambda i: (0, 0))],
        out_specs=pl.BlockSpec((tm, d), lambda i: (i, 0)),
        out_shape=jax.ShapeDtypeStruct((t, d), F32),
        scratch_shapes=[pltpu.VMEM((tm, d), F32), pltpu.VMEM((tm, d), F32),
                        pltpu.SemaphoreType.DMA],
        compiler_params=_params("arbitrary"),
        name="moe_combine_ln",
    )(dest1, dest2, info, y_buf, res, g, b)


def _pad_cols(w, n):
    return jnp.pad(w, ((0, 0), (0, n - w.shape[1])))


def _mamba_layer(x, xb, batch, seq, ln_g, ln_b, w_in, conv_w, conv_b, dt_bias, a_log, d_skip,
                 norm_w, w_out, w_gate, w_up, w_down):
    d_model = x.shape[1]
    n_heads = a_log.shape[0]
    d_inner = n_heads * SSM_HEAD_DIM
    conv_dim = conv_w.shape[1]
    w_z, w_xbc, w_dt = jnp.split(w_in, [d_inner, d_inner + conv_dim], axis=1)
    w_main = jnp.concatenate([w_xbc, w_z], axis=1).astype(BF16)
    zx = _matmul_resident(xb, w_main, BF16, min(ROW_TILE, x.shape[0]), 2 * ROW_TILE,
                          "ssm_in_proj", silu_from=conv_dim)
    dt_raw = _matmul(xb, _pad_cols(w_dt, LANES_V7X).astype(BF16), F32,
                     min(ROW_TILE * 2, x.shape[0]), LANES_V7X, "ssm_dt_proj")
    y = _ssd(zx, dt_raw, conv_w, conv_b, dt_bias, a_log, d_skip, norm_w, batch, seq, d_inner,
             n_heads)
    g0 = ln_g[0].reshape(1, d_model)
    b0 = ln_b[0].reshape(1, d_model)
    x1, x1b = _matmul_ln(y, w_out.astype(BF16), x, g0, b0, min(ROW_TILE, x.shape[0]),
                         "ssm_out_proj_ln")
    f = w_gate.shape[1]
    fp = -(-f // (2 * LANES_V7X)) * (2 * LANES_V7X)
    wg = _pad_cols(w_gate, fp).astype(BF16)[None]
    wu = _pad_cols(w_up, fp).astype(BF16)[None]
    wd = jnp.pad(w_down, ((0, fp - f), (0, 0))).astype(BF16)[None]
    g1 = ln_g[1].reshape(1, d_model)
    b1 = ln_b[1].reshape(1, d_model)
    return _ffn_ln(x1b, wg, wu, wd, x1, g1, b1, min(ROW_TILE, x.shape[0]), fp,
                   "dense_swiglu_ln")


def _attn_moe_layer(x, xb, batch, seq, ln_g, ln_b, w_qkv, w_o, w_router, b_router,
                    w_gate, w_up, w_down):
    t, d_model = x.shape
    qscale = jnp.concatenate([jnp.full((d_model,), LOG2E / math.sqrt(SB_HEAD_DIM), F32),
                              jnp.ones((2 * d_model,), F32)])
    w_qkv_b = (w_qkv * qscale).astype(BF16)
    qkv = _matmul_resident(xb, w_qkv_b, BF16, min(ROW_TILE * 2, t), d_model, "sb_qkv_proj")
    o = _attention(qkv, batch, seq, d_model)
    g0 = ln_g[0].reshape(1, d_model)
    b0 = ln_b[0].reshape(1, d_model)
    x1, _ = _matmul_ln(o, w_o.astype(BF16), x, g0, b0, min(ROW_TILE, t), "sb_out_proj_ln")

    info, cnt = _router(x1, w_router, b_router)
    counts = cnt[0, :N_EXPERTS].astype(jnp.int32)
    padded = (counts + MOE_ROWS - 1) // MOE_ROWS * MOE_ROWS
    pad_end = jnp.cumsum(padded)
    pad_start = pad_end - padded
    e1 = info[:, 0].astype(jnp.int32)
    e2 = info[:, 1].astype(jnp.int32)
    dest1 = pad_start[e1] + info[:, 4].astype(jnp.int32)
    dest2 = pad_start[e2] + info[:, 5].astype(jnp.int32)
    n_slots = 2 * t + N_EXPERTS * MOE_ROWS
    n_blocks = n_slots // MOE_ROWS
    block_start = jnp.arange(n_blocks) * MOE_ROWS
    block_expert = jnp.minimum(
        jnp.sum(pad_end[None, :] <= block_start[:, None], axis=1), N_EXPERTS - 1).astype(jnp.int32)
    n_used = (pad_end[-1] // MOE_ROWS).astype(jnp.int32).reshape(1)

    x_buf = _dispatch(x1, dest1, dest2, n_slots)
    f = w_gate.shape[2]
    y_buf = _ffn_experts(x_buf, block_expert, n_used, w_gate.astype(BF16), w_up.astype(BF16),
                         w_down.astype(BF16), MOE_ROWS, f // 2, "moe_expert_swiglu")
    g1 = ln_g[1].reshape(1, d_model)
    b1 = ln_b[1].reshape(1, d_model)
    return _combine_ln(y_buf, dest1, dest2, info, x1, g1, b1)


def kernel(x, ln_g, ln_b, ssm_w_in, ssm_conv_w, ssm_conv_b, ssm_dt_bias, ssm_a_log, ssm_d,
           ssm_norm_w, ssm_w_out, sb_w_qkv, sb_w_o, ffn_w_gate, ffn_w_up, ffn_w_down,
           moe_w_router, moe_b_router, moe_w_gate, moe_w_up, moe_w_down):
    batch, seq, d_model = x.shape
    xf = x.reshape(batch * seq, d_model)
    x1, x1b = _mamba_layer(xf, xf.astype(BF16), batch, seq, ln_g[0], ln_b[0], ssm_w_in[0],
                           ssm_conv_w[0], ssm_conv_b[0], ssm_dt_bias[0], ssm_a_log[0], ssm_d[0],
                           ssm_norm_w[0], ssm_w_out[0], ffn_w_gate[0], ffn_w_up[0], ffn_w_down[0])
    out = _attn_moe_layer(x1, x1b, batch, seq, ln_g[1], ln_b[1], sb_w_qkv[0], sb_w_o[0],
                          moe_w_router[0], moe_b_router[0], moe_w_gate[0], moe_w_up[0],
                          moe_w_down[0])
    return out.reshape(batch, seq, d_model)
```

```python
import functools
import math

import jax
import jax.numpy as jnp
from jax import lax
from jax.experimental import pallas as pl
from jax.experimental.pallas import tpu as pltpu

F32 = jnp.float32
BF16 = jnp.bfloat16

LANES_V7X = 128
SUBLANES_V7X = 8
VMEM_LIMIT_BYTES_V7X = 56 * 1024 * 1024

DEPTH = 2
SSM_HEAD_DIM = 64
SSM_GROUPS = 8
SSM_D_STATE = 128
SSM_CONV = 4
SSM_CHUNK = 128
CONV_HISTORY_ROWS = 16
SB_HEAD_DIM = 64
N_EXPERTS = 8
DEEPNORM_ALPHA = (2.0 * DEPTH) ** 0.25
LN_EPS = 1e-5
RMS_EPS = 1e-5
LOG2E = 1.4426950408889634
LN2 = 0.6931471805599453
F32_MIN_EXP2 = -150.0
SCORE_BOUND_SLACK = 1.01

ROW_TILE = 512
MOE_ROWS = 512
ATTN_Q_TILE = 256
ATTN_K_TILE = 128
ATTN_PAIRS_PER_STEP = 2
ATTN_SUB_ROWS = 128
ROUTE_TILE = 1024


def _params(*semantics):
    return pltpu.CompilerParams(dimension_semantics=semantics,
                                vmem_limit_bytes=VMEM_LIMIT_BYTES_V7X)


def _layer_norm_rows(y, g, b):
    mu = jnp.mean(y, axis=-1, keepdims=True)
    d = y - mu
    var = jnp.mean(d * d, axis=-1, keepdims=True)
    return d * lax.rsqrt(var + LN_EPS) * g + b


def _silu(x):
    h = 0.5 * x
    return h * jnp.tanh(h) + h


def _softplus(x):
    return jnp.maximum(x, 0.0) + jnp.log(1.0 + jnp.exp(-jnp.abs(x)))


def _bdot(a, b):
    return jnp.dot(a.astype(BF16), b.astype(BF16), preferred_element_type=F32)


def _mm_kernel(x_ref, w_ref, o_ref):
    o_ref[...] = jnp.dot(x_ref[...], w_ref[...],
                         preferred_element_type=F32).astype(o_ref.dtype)


def _matmul(x, w, out_dtype, tm, tn, name):
    m, k = x.shape
    n = w.shape[1]
    return pl.pallas_call(
        _mm_kernel,
        grid=(m // tm, n // tn),
        in_specs=[pl.BlockSpec((tm, k), lambda i, j: (i, 0)),
                  pl.BlockSpec((k, tn), lambda i, j: (0, j))],
        out_specs=pl.BlockSpec((tm, tn), lambda i, j: (i, j)),
        out_shape=jax.ShapeDtypeStruct((m, n), out_dtype),
        compiler_params=_params("parallel", "arbitrary"),
        name=name,
    )(x, w)


def _mm_resident_kernel(x_ref, w_ref, o_ref, *, chunk, silu_from):
    x = x_ref[...]
    for c in range(o_ref.shape[1] // chunk):
        cols = slice(c * chunk, (c + 1) * chunk)
        r = jnp.dot(x, w_ref[:, cols], preferred_element_type=F32)
        if silu_from is not None and c * chunk >= silu_from:
            r = _silu(r)
        o_ref[:, cols] = r.astype(o_ref.dtype)


def _matmul_resident(x, w, out_dtype, tm, chunk, name, silu_from=None):
    m, k = x.shape
    n = w.shape[1]
    return pl.pallas_call(
        functools.partial(_mm_resident_kernel, chunk=chunk, silu_from=silu_from),
        grid=(m // tm,),
        in_specs=[pl.BlockSpec((tm, k), lambda i: (i, 0)),
                  pl.BlockSpec((k, n), lambda i: (0, 0), pipeline_mode=pl.Buffered(1))],
        out_specs=pl.BlockSpec((tm, n), lambda i: (i, 0)),
        out_shape=jax.ShapeDtypeStruct((m, n), out_dtype),
        compiler_params=_params("parallel"),
        name=name,
    )(x, w)


def _mm_ln_kernel(a_ref, w_ref, res_ref, g_ref, b_ref, o_ref):
    h = jnp.dot(a_ref[...], w_ref[...], preferred_element_type=F32)
    o_ref[...] = _layer_norm_rows(DEEPNORM_ALPHA * res_ref[...] + h, g_ref[...], b_ref[...])


def _matmul_ln(a, w, res, g, b, tm, name):
    m, k = a.shape
    n = w.shape[1]
    return pl.pallas_call(
        _mm_ln_kernel,
        grid=(m // tm,),
        in_specs=[pl.BlockSpec((tm, k), lambda i: (i, 0)),
                  pl.BlockSpec((k, n), lambda i: (0, 0)),
                  pl.BlockSpec((tm, n), lambda i: (i, 0)),
                  pl.BlockSpec((1, n), lambda i: (0, 0)),
                  pl.BlockSpec((1, n), lambda i: (0, 0))],
        out_specs=pl.BlockSpec((tm, n), lambda i: (i, 0)),
        out_shape=jax.ShapeDtypeStruct((m, n), F32),
        compiler_params=_params("parallel"),
        name=name,
    )(a, w, res, g, b)


def _ffn_body(eid_ref, nused_ref, x_ref, wg_ref, wu_ref, wd_ref, acc_ref):
    j = pl.program_id(1)
    x = x_ref[...].astype(BF16)
    g = jnp.dot(x, wg_ref[0], preferred_element_type=F32)
    u = jnp.dot(x, wu_ref[0], preferred_element_type=F32)
    h = (_silu(g) * u).astype(BF16)
    part = jnp.dot(h, wd_ref[0], preferred_element_type=F32)

    @pl.when(j == 0)
    def _():
        acc_ref[...] = part

    @pl.when(j > 0)
    def _():
        acc_ref[...] += part


def _ffn_ln_kernel(eid_ref, nused_ref, x_ref, wg_ref, wu_ref, wd_ref, g_ref, b_ref,
                   o_ref, ob_ref, acc_ref):
    _ffn_body(eid_ref, nused_ref, x_ref, wg_ref, wu_ref, wd_ref, acc_ref)

    @pl.when(pl.program_id(1) == pl.num_programs(1) - 1)
    def _():
        o = _layer_norm_rows(DEEPNORM_ALPHA * x_ref[...] + acc_ref[...],
                             g_ref[...], b_ref[...])
        o_ref[...] = o
        ob_ref[...] = o.astype(BF16)


def _ffn_raw_kernel(eid_ref, nused_ref, x_ref, wg_ref, wu_ref, wd_ref, o_ref, acc_ref):
    i = pl.program_id(0)
    last = pl.program_id(1) == pl.num_programs(1) - 1
    used = i < nused_ref[0]

    @pl.when(used)
    def _():
        _ffn_body(eid_ref, nused_ref, x_ref, wg_ref, wu_ref, wd_ref, acc_ref)

    @pl.when(jnp.logical_and(used, last))
    def _():
        o_ref[...] = acc_ref[...]

    @pl.when(jnp.logical_and(jnp.logical_not(used), last))
    def _():
        o_ref[...] = jnp.zeros_like(o_ref)


def _ffn_specs(tm, d, tf, nf, fixed_weights=False):
    def wcol(i, j, eid, nused):
        return (eid[i], 0, jnp.where(i < nused[0], j, nf - 1))

    def wrow(i, j, eid, nused):
        return (eid[i], jnp.where(i < nused[0], j, nf - 1), 0)

    mode = dict(pipeline_mode=pl.Buffered(1)) if fixed_weights else {}
    return [pl.BlockSpec((tm, d), lambda i, j, eid, nused: (i, 0)),
            pl.BlockSpec((1, d, tf), wcol, **mode),
            pl.BlockSpec((1, d, tf), wcol, **mode),
            pl.BlockSpec((1, tf, d), wrow, **mode)]


def _ffn_ln(x, wg, wu, wd, g, b, tm, tf, name):
    m, d = x.shape
    f = wg.shape[2]
    nf = f // tf
    nblk = m // tm
    eid = jnp.zeros((nblk,), jnp.int32)
    nused = jnp.full((1,), nblk, jnp.int32)
    row = lambda i, j, eid, nused: (i, 0)
    vec = lambda i, j, eid, nused: (0, 0)
    return pl.pallas_call(
        _ffn_ln_kernel,
        grid_spec=pltpu.PrefetchScalarGridSpec(
            num_scalar_prefetch=2,
            grid=(nblk, nf),
            in_specs=_ffn_specs(tm, d, tf, nf, fixed_weights=(nf == 1)) + [
                pl.BlockSpec((1, d), vec),
                pl.BlockSpec((1, d), vec)],
            out_specs=[pl.BlockSpec((tm, d), row), pl.BlockSpec((tm, d), row)],
            scratch_shapes=[pltpu.VMEM((tm, d), F32)]),
        out_shape=[jax.ShapeDtypeStruct((m, d), F32),
                   jax.ShapeDtypeStruct((m, d), BF16)],
        compiler_params=_params("parallel", "arbitrary"),
        name=name,
    )(eid, nused, x, wg, wu, wd, g, b)


def _ffn_experts(x_buf, eid, nused, wg, wu, wd, tm, tf, name):
    m, d = x_buf.shape
    f = wg.shape[2]
    nf = f // tf
    nblk = m // tm
    row = lambda i, j, eid, nused: (i, 0)
    return pl.pallas_call(
        _ffn_raw_kernel,
        grid_spec=pltpu.PrefetchScalarGridSpec(
            num_scalar_prefetch=2,
            grid=(nblk, nf),
            in_specs=_ffn_specs(tm, d, tf, nf),
            out_specs=pl.BlockSpec((tm, d), row),
            scratch_shapes=[pltpu.VMEM((tm, d), F32)]),
        out_shape=jax.ShapeDtypeStruct((m, d), F32),
        compiler_params=_params("parallel", "arbitrary"),
        name=name,
    )(eid, nused, x_buf, wg, wu, wd)


def _split2(v):
    hi = v.astype(BF16)
    lo = (v - hi.astype(F32)).astype(BF16)
    return hi, lo


def _expand_cols(parts, e):
    out = jnp.dot(parts[0], e, preferred_element_type=F32)
    for p in parts[1:]:
        out += jnp.dot(p, e, preferred_element_type=F32)
    return out


def _ssd_kernel(xbc_ref, gate_ref, dt_ref, convw_ref, convb_ref, shift_ref, dtb_ref, alog_ref,
                dexp_ref, normw_ref, ltri_ref, sel128_ref, sel64_ref, y_ref,
                ext_ref, act_ref, eoff_ref, state_ref, *, d_inner, n_heads):
    L = SSM_CHUNK
    N = SSM_D_STATE
    G = SSM_GROUPS
    hpg = n_heads // G
    gw = d_inner // G
    hist = CONV_HISTORY_ROWS

    @pl.when(pl.program_id(1) == 0)
    def _():
        ext_ref[0:hist, :] = jnp.zeros((hist, ext_ref.shape[1]), BF16)
        state_ref[...] = jnp.zeros_like(state_ref)

    dt = _softplus(dt_ref[...] + dtb_ref[...])
    a = -jnp.exp(alog_ref[...])
    acum = jnp.dot(ltri_ref[...], dt * a, preferred_element_type=F32,
                   precision=lax.Precision.HIGHEST)
    acum_last = acum[L - 1:L, :]
    acum_t = acum.T
    dt_t = dt.T
    acum_parts = _split2(acum)

    ext_ref[hist:hist + L, :] = xbc_ref[...]
    shifted = jnp.dot(shift_ref[...], ext_ref[...], preferred_element_type=F32)
    conv = convb_ref[...] + convw_ref[SSM_CONV - 1:SSM_CONV, :] * xbc_ref[...].astype(F32)
    for k in range(SSM_CONV - 1):
        conv = conv + convw_ref[k:k + 1, :] * shifted[k * L:(k + 1) * L, :]
    ext_ref[0:hist, :] = ext_ref[L:L + hist, :]
    act_ref[...] = _silu(conv)

    eoff_ref[...] = _expand_cols(_split2(jnp.exp(acum)), sel64_ref[...])
    eend = _expand_cols(_split2(jnp.exp(acum_last - acum) * dt), sel64_ref[...])

    row = lax.broadcasted_iota(jnp.int32, (L, L), 0)
    col = lax.broadcasted_iota(jnp.int32, (L, L), 1)
    causal = col <= row
    glane = lax.broadcasted_iota(jnp.int32, (L, gw), 1) // SSM_HEAD_DIM

    for g in range(G):
        xs_g = act_ref[:, g * gw:(g + 1) * gw]
        xs_b = xs_g.astype(BF16)
        b_g = act_ref[:, d_inner + g * N:d_inner + (g + 1) * N]
        c_gb = act_ref[:, d_inner + G * N + g * N:d_inner + G * N + (g + 1) * N].astype(BF16)
        b_gt = b_g.T.astype(BF16)
        cb = jnp.dot(c_gb, b_gt, preferred_element_type=F32)
        acol = _expand_cols(acum_parts,
                            sel128_ref[:, g * hpg * L:(g + 1) * hpg * L])
        ws = []
        xs_heads = []
        for h in range(hpg):
            hh = g * hpg + h
            seg = acol[:, h * L:(h + 1) * L] - acum_t[hh:hh + 1, :]
            decay = jnp.exp(jnp.where(causal, seg, -jnp.inf))
            ws.append((cb * decay * dt_t[hh:hh + 1, :]).astype(BF16))
            xs_heads.append(jnp.where(glane == h, xs_b, jnp.zeros_like(xs_b)))
        y_g = jnp.dot(jnp.concatenate(ws, axis=1), jnp.concatenate(xs_heads, axis=0),
                      preferred_element_type=F32)
        state_g = state_ref[:, g * gw:(g + 1) * gw]
        eoff_g = eoff_ref[:, g * gw:(g + 1) * gw]
        y_g = y_g + _bdot(c_gb, state_g) * eoff_g
        xs_scaled = xs_g * eend[:, g * gw:(g + 1) * gw]
        state_ref[:, g * gw:(g + 1) * gw] = (
            state_g * eoff_g[L - 1:L, :]
            + jnp.dot(b_gt, xs_scaled.astype(BF16), preferred_element_type=F32))
        y_g = y_g + dexp_ref[:, g * gw:(g + 1) * gw] * xs_g
        y_g = y_g * gate_ref[:, g * gw:(g + 1) * gw].astype(F32)
        ms = jnp.mean(y_g * y_g, axis=-1, keepdims=True)
        y_g = y_g * lax.rsqrt(ms + RMS_EPS) * normw_ref[:, g * gw:(g + 1) * gw]
        y_ref[:, g * gw:(g + 1) * gw] = y_g.astype(y_ref.dtype)


def _ssd(zx, dt_raw, conv_w, conv_b, dt_bias, a_log, d_skip, norm_w, batch, seq, d_inner,
         n_heads):
    L = SSM_CHUNK
    conv_dim = d_inner + 2 * SSM_GROUPS * SSM_D_STATE
    nc = seq // L
    pad = LANES_V7X - n_heads
    dtb = jnp.pad(dt_bias, (0, pad)).reshape(1, LANES_V7X)
    alog = jnp.pad(a_log, (0, pad)).reshape(1, LANES_V7X)
    dexp = jnp.repeat(d_skip, SSM_HEAD_DIM).reshape(1, d_inner)
    ltri = jnp.tril(jnp.ones((L, L), F32))
    hidx = jnp.arange(LANES_V7X)[:, None]
    sel128 = (hidx == (jnp.arange(n_heads * L) // L)[None, :]).astype(BF16)
    sel64 = (hidx == (jnp.arange(d_inner) // SSM_HEAD_DIM)[None, :]).astype(BF16)
    z_blk = conv_dim // d_inner
    t_idx = jnp.arange((SSM_CONV - 1) * L)
    src = t_idx % L + CONV_HISTORY_ROWS - (SSM_CONV - 1) + t_idx // L
    shift = (src[:, None] == jnp.arange(L + CONV_HISTORY_ROWS)[None, :]).astype(BF16)
    const = lambda b, c: (0, 0)
    kern = functools.partial(_ssd_kernel, d_inner=d_inner, n_heads=n_heads)
    return pl.pallas_call(
        kern,
        grid=(batch, nc),
        in_specs=[pl.BlockSpec((L, conv_dim), lambda b, c: (b * nc + c, 0)),
                  pl.BlockSpec((L, d_inner), lambda b, c: (b * nc + c, z_blk)),
                  pl.BlockSpec((L, LANES_V7X), lambda b, c: (b * nc + c, 0)),
                  pl.BlockSpec((SSM_CONV, conv_dim), const),
                  pl.BlockSpec((1, conv_dim), const),
                  pl.BlockSpec(((SSM_CONV - 1) * L, L + CONV_HISTORY_ROWS), const),
                  pl.BlockSpec((1, LANES_V7X), const),
                  pl.BlockSpec((1, LANES_V7X), const),
                  pl.BlockSpec((1, d_inner), const),
                  pl.BlockSpec((1, d_inner), const),
                  pl.BlockSpec((L, L), const),
                  pl.BlockSpec((LANES_V7X, n_heads * L), const),
                  pl.BlockSpec((LANES_V7X, d_inner), const)],
        out_specs=pl.BlockSpec((L, d_inner), lambda b, c: (b * nc + c, 0)),
        out_shape=jax.ShapeDtypeStruct((batch * seq, d_inner), BF16),
        scratch_shapes=[pltpu.VMEM((L + CONV_HISTORY_ROWS, conv_dim), BF16),
                        pltpu.VMEM((L, conv_dim), F32),
                        pltpu.VMEM((L, d_inner), F32),
                        pltpu.VMEM((SSM_D_STATE, d_inner), F32)],
        compiler_params=_params("parallel", "arbitrary"),
        name="ssd_scan",
    )(zx, zx, dt_raw, conv_w, conv_b.reshape(1, conv_dim), shift, dtb, alog, dexp,
      norm_w.reshape(1, d_inner), ltri, sel128, sel64)


def _attn_kernel(q_ref, k_ref, v_ref, mbd_ref, o_ref, kk_ref, vv_ref, acc_ref, kn_ref, z_ref,
                 *, tq):
    tk = ATTN_K_TILE
    qi = pl.program_id(2)
    n_kb = k_ref.shape[0] // tk
    n_diag = tq // tk
    lane = lax.broadcasted_iota(jnp.int32, (tk, LANES_V7X), 1)
    head0 = lane < SB_HEAD_DIM

    n_pairs = q_ref.shape[1] // LANES_V7X
    rows = SUBLANES_V7X

    @pl.when(qi == 0)
    def _():
        for c in range(n_pairs):
            cols = slice(c * LANES_V7X, (c + 1) * LANES_V7X)

            def fill(j, carry, c=c, cols=cols):
                m0, m1 = carry
                kb = k_ref[pl.ds(pl.multiple_of(j * tk, tk), tk), cols]
                vb = v_ref[pl.ds(pl.multiple_of(j * tk, tk), tk), cols]
                zero = jnp.zeros_like(kb)
                kk_ref[c, j, 0:tk, :] = jnp.where(head0, kb, zero)
                kk_ref[c, j, tk:2 * tk, :] = jnp.where(head0, zero, kb)
                vv_ref[c, j, 0:tk, :] = jnp.where(head0, vb, zero)
                vv_ref[c, j, tk:2 * tk, :] = jnp.where(head0, zero, vb)
                ksq = kb.astype(F32) * kb.astype(F32)
                n0 = jnp.sum(jnp.where(head0, ksq, 0.0), axis=-1, keepdims=True)
                n1 = jnp.sum(jnp.where(head0, 0.0, ksq), axis=-1, keepdims=True)
                return jnp.maximum(m0, n0), jnp.maximum(m1, n1)
            init = (jnp.zeros((tk, 1), F32), jnp.zeros((tk, 1), F32))
            m0, m1 = lax.fori_loop(0, n_kb, fill, init, unroll=4)
            kn_ref[(2 * c) * rows:(2 * c + 1) * rows, :] = jnp.broadcast_to(
                jnp.max(m0, axis=0, keepdims=True), (rows, LANES_V7X))
            kn_ref[(2 * c + 1) * rows:(2 * c + 2) * rows, :] = jnp.broadcast_to(
                jnp.max(m1, axis=0, keepdims=True), (rows, LANES_V7X))

    acc_ref[...] = jnp.zeros_like(acc_ref)
    sub = min(ATTN_SUB_ROWS, tq)
    n_sub = tq // sub
    s_off = lax.broadcasted_iota(jnp.int32, (sub, 2 * tk), 1) % tk
    row = lax.broadcasted_iota(jnp.int32, (sub, 2 * tk), 0)
    qhead0 = lax.broadcasted_iota(jnp.int32, (tq, LANES_V7X), 1) < SB_HEAD_DIM

    chains = [(c, s) for c in range(n_pairs) for s in range(n_sub)]
    qs = {}
    zbmax = []
    for c in range(n_pairs):
        q = q_ref[:, c * LANES_V7X:(c + 1) * LANES_V7X]
        qsq = q.astype(F32) * q.astype(F32)
        zb_pair = []
        for h, sel in enumerate((qhead0, jnp.logical_not(qhead0))):
            qn = jnp.max(jnp.sum(jnp.where(sel, qsq, 0.0), axis=-1, keepdims=True),
                         axis=0, keepdims=True)
            kn = kn_ref[(2 * c + h) * rows:(2 * c + h) * rows + 1, 0:1]
            zb_pair.append(jnp.sqrt(qn * kn) * SCORE_BOUND_SLACK + SCORE_BOUND_SLACK)
        for s in range(n_sub):
            qs[(c, s)] = q[s * sub:(s + 1) * sub, :]
            zbmax.extend(zb_pair)

    def scores(n, j):
        return lax.dot_general(qs[chains[n]], kk_ref[chains[n][0], j], (((1,), (1,)), ((), ())),
                               preferred_element_type=F32)

    def run(jobs, accs, zs=None, nxt=None):
        old = list(accs)
        new = list(accs)
        if zs is None:
            zs = [scores(n, j) for n, j, _ in jobs]
        sps = []
        for z, (n, _, mask) in zip(zs, jobs):
            sp = jnp.maximum(z, 0.0) + jnp.log2(1.0 + jnp.exp2(-jnp.abs(z)))
            sp = sp if mask is None else jnp.where(mask, sp, 0.0)
            sps.append(sp)
            new[2 * n] = old[2 * n] + jnp.sum(sp[:, :tk], axis=-1, keepdims=True)
            new[2 * n + 1] = old[2 * n + 1] + jnp.sum(sp[:, tk:], axis=-1, keepdims=True)
        incls = [jnp.dot(sp.astype(BF16), mbd_ref[...], preferred_element_type=F32)
                 for sp in sps]
        if nxt is not None:
            for n, _, _ in jobs:
                z_ref[n] = scores(n, nxt)
        ps = []
        for z, incl, (n, _, mask) in zip(zs, incls, jobs):
            la0 = z[:, :tk] - incl[:, :tk] - old[2 * n]
            la1 = z[:, tk:] - incl[:, tk:] - old[2 * n + 1]
            p = jnp.concatenate([jnp.exp2(la0), jnp.exp2(la1)], axis=1)
            ps.append((p if mask is None else jnp.where(mask, p, 0.0)).astype(BF16))
        for p, (n, j, _) in zip(ps, jobs):
            c, s = chains[n]
            acc_ref[s * sub:(s + 1) * sub, c * LANES_V7X:(c + 1) * LANES_V7X] += jnp.dot(
                p, vv_ref[c, j], preferred_element_type=F32)
        return new

    def live(accs):
        worst = zbmax[0] - jnp.min(accs[0], axis=0, keepdims=True)
        for zb, a in zip(zbmax[1:], accs[1:]):
            worst = jnp.maximum(worst, zb - jnp.min(a, axis=0, keepdims=True))
        return (jnp.max(worst) > F32_MIN_EXP2).astype(jnp.int32)

    accs = [jnp.zeros((sub, 1), F32) for _ in range(2 * len(chains))]
    j_diag = qi * n_diag
    for d in range(n_diag - 1, -1, -1):
        jobs = []
        for n, (c, s) in enumerate(chains):
            if d * tk >= (s + 1) * sub - 1:
                continue
            all_visible = (d + 1) * tk <= s * sub
            jobs.append((n, j_diag + d, None if all_visible else d * tk + s_off < s * sub + row))
        accs = run(jobs, accs, nxt=jnp.maximum(j_diag - 1, 0) if d == 0 else None)

    def cond(carry):
        return jnp.logical_and(carry[0] < j_diag, carry[1] > 0)

    def body(carry):
        j = j_diag - 1 - carry[0]
        zs = [z_ref[n] for n in range(len(chains))]
        accs = run([(n, j, None) for n in range(len(chains))], carry[2:], zs=zs,
                   nxt=jnp.maximum(j - 1, 0))
        return (carry[0] + 1, live(accs)) + tuple(accs)

    lax.while_loop(cond, body, (jnp.int32(0), live(accs)) + tuple(accs))
    o_ref[...] = acc_ref[...].astype(o_ref.dtype)


def _attention(qkv, batch, seq, d_model):
    tq = min(ATTN_Q_TILE, seq)
    tk = ATTN_K_TILE
    nq = seq // tq
    width = ATTN_PAIRS_PER_STEP * LANES_V7X
    n_groups = d_model // width
    r = jnp.arange(2 * tk)
    mbd = ((r[:, None] // tk == r[None, :] // tk) & (r[:, None] >= r[None, :])).astype(BF16)
    kern = functools.partial(_attn_kernel, tq=tq)
    return pl.pallas_call(
        kern,
        grid=(batch, n_groups, nq),
        in_specs=[pl.BlockSpec((tq, width), lambda b, p, i: (b * nq + i, p)),
                  pl.BlockSpec((seq, width), lambda b, p, i: (b, n_groups + p)),
                  pl.BlockSpec((seq, width), lambda b, p, i: (b, 2 * n_groups + p)),
                  pl.BlockSpec((2 * tk, 2 * tk), lambda b, p, i: (0, 0))],
        out_specs=pl.BlockSpec((tq, width), lambda b, p, i: (b * nq + i, p)),
        out_shape=jax.ShapeDtypeStruct((batch * seq, d_model), BF16),
        scratch_shapes=[pltpu.VMEM((ATTN_PAIRS_PER_STEP, seq // tk, 2 * tk, LANES_V7X), BF16),
                        pltpu.VMEM((ATTN_PAIRS_PER_STEP, seq // tk, 2 * tk, LANES_V7X), BF16),
                        pltpu.VMEM((tq, width), F32),
                        pltpu.VMEM((2 * ATTN_PAIRS_PER_STEP * SUBLANES_V7X, LANES_V7X), F32),
                        pltpu.VMEM((ATTN_PAIRS_PER_STEP * (tq // min(ATTN_SUB_ROWS, tq)),
                                    min(ATTN_SUB_ROWS, tq), 2 * tk), F32)],
        compiler_params=_params("parallel", "parallel", "arbitrary"),
        name="stickbreak_attn",
    )(qkv, qkv, qkv, mbd)


def _router_kernel(x_ref, w_ref, b_ref, ltri_ref, info_ref, cnt_ref, carry_ref):
    tm = x_ref.shape[0]

    @pl.when(pl.program_id(0) == 0)
    def _():
        carry_ref[...] = jnp.zeros_like(carry_ref)

    logits = jnp.dot(x_ref[...], w_ref[...], preferred_element_type=F32,
                     precision=lax.Precision.HIGHEST) + b_ref[...]
    lane = lax.broadcasted_iota(jnp.int32, (tm, LANES_V7X), 1).astype(F32)
    big = float(LANES_V7X)
    m1 = jnp.max(logits, axis=-1, keepdims=True)
    i1 = jnp.min(jnp.where(logits == m1, lane, big), axis=-1, keepdims=True)
    l2 = jnp.where(lane == i1, -jnp.inf, logits)
    m2 = jnp.max(l2, axis=-1, keepdims=True)
    i2 = jnp.min(jnp.where(l2 == m2, lane, big), axis=-1, keepdims=True)
    e = jnp.exp(m2 - m1)
    g1 = 1.0 / (1.0 + e)
    g2 = e * g1
    hit1 = lane == i1
    hit2 = lane == i2
    oh = jnp.where(jnp.logical_or(hit1, hit2), 1.0, 0.0)
    cum = jnp.dot(ltri_ref[...], oh.astype(BF16), preferred_element_type=F32)
    before = cum - oh + carry_ref[0:1, :]
    r1 = jnp.sum(jnp.where(hit1, before, 0.0), axis=-1, keepdims=True)
    r2 = jnp.sum(jnp.where(hit2, before, 0.0), axis=-1, keepdims=True)
    total = carry_ref[0:1, :] + cum[tm - 1:tm, :]
    carry_ref[...] = jnp.broadcast_to(total, carry_ref.shape)
    cnt_ref[...] = jnp.broadcast_to(total, cnt_ref.shape)
    info = jnp.where(lane == 0.0, i1, 0.0)
    info = jnp.where(lane == 1.0, i2, info)
    info = jnp.where(lane == 2.0, g1, info)
    info = jnp.where(lane == 3.0, g2, info)
    info = jnp.where(lane == 4.0, r1, info)
    info = jnp.where(lane == 5.0, r2, info)
    info_ref[...] = info


def _router(x, w_router, b_router):
    t, d = x.shape
    tm = min(ROUTE_TILE, t)
    pad = LANES_V7X - N_EXPERTS
    w = jnp.pad(w_router, ((0, 0), (0, pad)))
    b = jnp.concatenate([b_router.astype(F32), jnp.full((pad,), -1e30, F32)]).reshape(1, LANES_V7X)
    ltri = jnp.tril(jnp.ones((tm, tm), BF16))
    return pl.pallas_call(
        _router_kernel,
        grid=(t // tm,),
        in_specs=[pl.BlockSpec((tm, d), lambda i: (i, 0)),
                  pl.BlockSpec((d, LANES_V7X), lambda i: (0, 0)),
                  pl.BlockSpec((1, LANES_V7X), lambda i: (0, 0)),
                  pl.BlockSpec((tm, tm), lambda i: (0, 0))],
        out_specs=[pl.BlockSpec((tm, LANES_V7X), lambda i: (i, 0)),
                   pl.BlockSpec((SUBLANES_V7X, LANES_V7X), lambda i: (0, 0))],
        out_shape=[jax.ShapeDtypeStruct((t, LANES_V7X), F32),
                   jax.ShapeDtypeStruct((SUBLANES_V7X, LANES_V7X), F32)],
        scratch_shapes=[pltpu.VMEM((SUBLANES_V7X, LANES_V7X), F32)],
        compiler_params=_params("arbitrary"),
        name="moe_router",
    )(x, w, b, ltri)


def _row_copy(src_ref, src_row, dst_ref, dst_row, sem):
    return pltpu.make_async_copy(src_ref.at[pl.ds(src_row, 1)],
                                 dst_ref.at[pl.ds(dst_row, 1)], sem)


def _dispatch_kernel(d1_ref, d2_ref, x_ref, buf_in_hbm, buf_hbm, sem):
    del buf_in_hbm
    tm = d1_ref.shape[0]

    def issue(t, c):
        _row_copy(x_ref, t, buf_hbm, d1_ref[t], sem).start()
        _row_copy(x_ref, t, buf_hbm, d2_ref[t], sem).start()
        return c
    lax.fori_loop(0, tm, issue, 0, unroll=8)

    def drain(t, c):
        _row_copy(x_ref, 0, buf_hbm, 0, sem).wait()
        _row_copy(x_ref, 0, buf_hbm, 0, sem).wait()
        return c
    lax.fori_loop(0, tm, drain, 0, unroll=8)


def _dispatch(x, dest1, dest2, n_slots):
    t, d = x.shape
    tm = min(ROUTE_TILE, t)
    buf0 = jnp.zeros((n_slots, d), x.dtype)
    smem = lambda: pl.BlockSpec((tm,), lambda i: (i,), memory_space=pltpu.SMEM)
    return pl.pallas_call(
        _dispatch_kernel,
        grid=(t // tm,),
        in_specs=[smem(), smem(),
                  pl.BlockSpec((tm, d), lambda i: (i, 0)),
                  pl.BlockSpec(memory_space=pl.ANY)],
        out_specs=pl.BlockSpec(memory_space=pl.ANY),
        out_shape=jax.ShapeDtypeStruct((n_slots, d), x.dtype),
        scratch_shapes=[pltpu.SemaphoreType.DMA],
        input_output_aliases={3: 0},
        compiler_params=_params("arbitrary"),
        name="moe_dispatch",
    )(dest1, dest2, x, buf0)


def _combine_kernel(d1_ref, d2_ref, info_ref, y_hbm, res_ref, g_ref, b_ref, o_ref,
                    buf1_ref, buf2_ref, sem):
    tm = d1_ref.shape[0]

    def issue(t, c):
        _row_copy(y_hbm, d1_ref[t], buf1_ref, t, sem).start()
        _row_copy(y_hbm, d2_ref[t], buf2_ref, t, sem).start()
        return c
    lax.fori_loop(0, tm, issue, 0, unroll=8)

    def drain(t, c):
        _row_copy(y_hbm, 0, buf1_ref, 0, sem).wait()
        _row_copy(y_hbm, 0, buf2_ref, 0, sem).wait()
        return c
    lax.fori_loop(0, tm, drain, 0, unroll=8)

    g1 = info_ref[:, 2:3]
    g2 = info_ref[:, 3:4]
    y = g1 * buf1_ref[...] + g2 * buf2_ref[...]
    o_ref[...] = _layer_norm_rows(DEEPNORM_ALPHA * res_ref[...] + y, g_ref[...], b_ref[...])


def _combine_ln(y_buf, dest1, dest2, info, res, g, b):
    t, d = res.shape
    tm = min(ROUTE_TILE, t)
    smem = lambda: pl.BlockSpec((tm,), lambda i: (i,), memory_space=pltpu.SMEM)
    return pl.pallas_call(
        _combine_kernel,
        grid=(t // tm,),
        in_specs=[smem(), smem(),
                  pl.BlockSpec((tm, LANES_V7X), lambda i: (i, 0)),
                  pl.BlockSpec(memory_space=pl.ANY),
                  pl.BlockSpec((tm, d), lambda i: (i, 0)),
                  pl.BlockSpec((1, d), lambda i: (0, 0)),
                  pl.BlockSpec((1, d), lambda i: (0, 0))],
        out_specs=pl.BlockSpec((tm, d), lambda i: (i, 0)),
        out_shape=jax.ShapeDtypeStruct((t, d), F32),
        scratch_shapes=[pltpu.VMEM((tm, d), F32), pltpu.VMEM((tm, d), F32),
                        pltpu.SemaphoreType.DMA],
        compiler_params=_params("arbitrary"),
        name="moe_combine_ln",
    )(dest1, dest2, info, y_buf, res, g, b)


def _pad_cols(w, n):
    return jnp.pad(w, ((0, 0), (0, n - w.shape[1])))


def _mamba_layer(x, xb, batch, seq, ln_g, ln_b, w_in, conv_w, conv_b, dt_bias, a_log, d_skip,
                 norm_w, w_out, w_gate, w_up, w_down):
    d_model = x.shape[1]
    n_heads = a_log.shape[0]
    d_inner = n_heads * SSM_HEAD_DIM
    conv_dim = conv_w.shape[1]
    w_z, w_xbc, w_dt = jnp.split(w_in, [d_inner, d_inner + conv_dim], axis=1)
    w_main = jnp.concatenate([w_xbc, w_z], axis=1).astype(BF16)
    zx = _matmul_resident(xb, w_main, BF16, min(ROW_TILE, x.shape[0]), 2 * ROW_TILE,
                          "ssm_in_proj", silu_from=conv_dim)
    dt_raw = _matmul(xb, _pad_cols(w_dt, LANES_V7X).astype(BF16), F32,
                     min(ROW_TILE * 2, x.shape[0]), LANES_V7X, "ssm_dt_proj")
    y = _ssd(zx, dt_raw, conv_w, conv_b, dt_bias, a_log, d_skip, norm_w, batch, seq, d_inner,
             n_heads)
    g0 = ln_g[0].reshape(1, d_model)
    b0 = ln_b[0].reshape(1, d_model)
    x1 = _matmul_ln(y, w_out.astype(BF16), x, g0, b0, min(ROW_TILE, x.shape[0]),
                         "ssm_out_proj_ln")
    f = w_gate.shape[1]
    fp = -(-f // (2 * LANES_V7X)) * (2 * LANES_V7X)
    wg = _pad_cols(w_gate, fp).astype(BF16)[None]
    wu = _pad_cols(w_up, fp).astype(BF16)[None]
    wd = jnp.pad(w_down, ((0, fp - f), (0, 0))).astype(BF16)[None]
    g1 = ln_g[1].reshape(1, d_model)
    b1 = ln_b[1].reshape(1, d_model)
    return _ffn_ln(x1, wg, wu, wd, g1, b1, min(ROW_TILE, x.shape[0]), fp,
                   "dense_swiglu_ln")


def _attn_moe_layer(x, xb, batch, seq, ln_g, ln_b, w_qkv, w_o, w_router, b_router,
                    w_gate, w_up, w_down):
    t, d_model = x.shape
    qscale = jnp.concatenate([jnp.full((d_model,), LOG2E / math.sqrt(SB_HEAD_DIM), F32),
                              jnp.ones((2 * d_model,), F32)])
    w_qkv_b = (w_qkv * qscale).astype(BF16)
    qkv = _matmul_resident(xb, w_qkv_b, BF16, min(ROW_TILE * 2, t), d_model, "sb_qkv_proj")
    o = _attention(qkv, batch, seq, d_model)
    g0 = ln_g[0].reshape(1, d_model)
    b0 = ln_b[0].reshape(1, d_model)
    x1 = _matmul_ln(o, w_o.astype(BF16), x, g0, b0, min(ROW_TILE, t), "sb_out_proj_ln")

    info, cnt = _router(x1, w_router, b_router)
    counts = cnt[0, :N_EXPERTS].astype(jnp.int32)
    padded = (counts + MOE_ROWS - 1) // MOE_ROWS * MOE_ROWS
    pad_end = jnp.cumsum(padded)
    pad_start = pad_end - padded
    e1 = info[:, 0].astype(jnp.int32)
    e2 = info[:, 1].astype(jnp.int32)
    dest1 = pad_start[e1] + info[:, 4].astype(jnp.int32)
    dest2 = pad_start[e2] + info[:, 5].astype(jnp.int32)
    n_slots = 2 * t + N_EXPERTS * MOE_ROWS
    n_blocks = n_slots // MOE_ROWS
    block_start = jnp.arange(n_blocks) * MOE_ROWS
    block_expert = jnp.minimum(
        jnp.sum(pad_end[None, :] <= block_start[:, None], axis=1), N_EXPERTS - 1).astype(jnp.int32)
    n_used = (pad_end[-1] // MOE_ROWS).astype(jnp.int32).reshape(1)

    x_buf = _dispatch(x1, dest1, dest2, n_slots)
    f = w_gate.shape[2]
    y_buf = _ffn_experts(x_buf, block_expert, n_used, w_gate.astype(BF16), w_up.astype(BF16),
                         w_down.astype(BF16), MOE_ROWS, f // 2, "moe_expert_swiglu")
    g1 = ln_g[1].reshape(1, d_model)
    b1 = ln_b[1].reshape(1, d_model)
    return _combine_ln(y_buf, dest1, dest2, info, x1, g1, b1)


def kernel(x, ln_g, ln_b, ssm_w_in, ssm_conv_w, ssm_conv_b, ssm_dt_bias, ssm_a_log, ssm_d,
           ssm_norm_w, ssm_w_out, sb_w_qkv, sb_w_o, ffn_w_gate, ffn_w_up, ffn_w_down,
           moe_w_router, moe_b_router, moe_w_gate, moe_w_up, moe_w_down):
    batch, seq, d_model = x.shape
    xf = x.reshape(batch * seq, d_model)
    x1, x1b = _mamba_layer(xf, xf.astype(BF16), batch, seq, ln_g[0], ln_b[0], ssm_w_in[0],
                           ssm_conv_w[0], ssm_conv_b[0], ssm_dt_bias[0], ssm_a_log[0], ssm_d[0],
                           ssm_norm_w[0], ssm_w_out[0], ffn_w_gate[0], ffn_w_up[0], ffn_w_down[0])
    out = _attn_moe_layer(x1, x1b, batch, seq, ln_g[1], ln_b[1], sb_w_qkv[0], sb_w_o[0],
                          moe_w_router[0], moe_b_router[0], moe_w_gate[0], moe_w_up[0],
                          moe_w_down[0])
    return out.reshape(batch, seq, d_model)
```

```python
import functools
import math

import jax
import jax.numpy as jnp
from jax import lax
from jax.experimental import pallas as pl
from jax.experimental.pallas import tpu as pltpu

F32 = jnp.float32
BF16 = jnp.bfloat16

LANES_V7X = 128
SUBLANES_V7X = 8
VMEM_LIMIT_BYTES_V7X = 56 * 1024 * 1024

DEPTH = 2
SSM_HEAD_DIM = 64
SSM_GROUPS = 8
SSM_D_STATE = 128
SSM_CONV = 4
SSM_CHUNK = 128
CONV_HISTORY_ROWS = 16
SSD_CHUNKS_PER_STEP = 4
SB_HEAD_DIM = 64
N_EXPERTS = 8
DEEPNORM_ALPHA = (2.0 * DEPTH) ** 0.25
LN_EPS = 1e-5
RMS_EPS = 1e-5
LOG2E = 1.4426950408889634
LN2 = 0.6931471805599453
F32_MIN_EXP2 = -150.0
SCORE_BOUND_SLACK = 1.01

ROW_TILE = 512
MOE_ROWS = 512
ATTN_Q_TILE = 256
ATTN_K_TILE = 128
ATTN_PAIRS_PER_STEP = 2
ATTN_SUB_ROWS = 128
ROUTE_TILE = 1024


def _params(*semantics):
    return pltpu.CompilerParams(dimension_semantics=semantics,
                                vmem_limit_bytes=VMEM_LIMIT_BYTES_V7X)


def _layer_norm_rows(y, g, b):
    mu = jnp.mean(y, axis=-1, keepdims=True)
    d = y - mu
    var = jnp.mean(d * d, axis=-1, keepdims=True)
    return d * lax.rsqrt(var + LN_EPS) * g + b


def _silu(x):
    h = 0.5 * x
    return h * jnp.tanh(h) + h


def _softplus(x):
    return jnp.maximum(x, 0.0) + jnp.log(1.0 + jnp.exp(-jnp.abs(x)))


def _bdot(a, b):
    return jnp.dot(a.astype(BF16), b.astype(BF16), preferred_element_type=F32)


def _mm_kernel(x_ref, w_ref, o_ref):
    o_ref[...] = jnp.dot(x_ref[...], w_ref[...],
                         preferred_element_type=F32).astype(o_ref.dtype)


def _matmul(x, w, out_dtype, tm, tn, name):
    m, k = x.shape
    n = w.shape[1]
    return pl.pallas_call(
        _mm_kernel,
        grid=(m // tm, n // tn),
        in_specs=[pl.BlockSpec((tm, k), lambda i, j: (i, 0)),
                  pl.BlockSpec((k, tn), lambda i, j: (0, j))],
        out_specs=pl.BlockSpec((tm, tn), lambda i, j: (i, j)),
        out_shape=jax.ShapeDtypeStruct((m, n), out_dtype),
        compiler_params=_params("parallel", "arbitrary"),
        name=name,
    )(x, w)


def _mm_resident_kernel(x_ref, w_ref, o_ref, *, chunk, silu_from):
    x = x_ref[...]
    for c in range(o_ref.shape[1] // chunk):
        cols = slice(c * chunk, (c + 1) * chunk)
        r = jnp.dot(x, w_ref[:, cols], preferred_element_type=F32)
        if silu_from is not None and c * chunk >= silu_from:
            r = _silu(r)
        o_ref[:, cols] = r.astype(o_ref.dtype)


def _matmul_resident(x, w, out_dtype, tm, chunk, name, silu_from=None):
    m, k = x.shape
    n = w.shape[1]
    return pl.pallas_call(
        functools.partial(_mm_resident_kernel, chunk=chunk, silu_from=silu_from),
        grid=(m // tm,),
        in_specs=[pl.BlockSpec((tm, k), lambda i: (i, 0)),
                  pl.BlockSpec((k, n), lambda i: (0, 0), pipeline_mode=pl.Buffered(1))],
        out_specs=pl.BlockSpec((tm, n), lambda i: (i, 0)),
        out_shape=jax.ShapeDtypeStruct((m, n), out_dtype),
        compiler_params=_params("parallel"),
        name=name,
    )(x, w)


def _mm_ln_kernel(a_ref, w_ref, res_ref, g_ref, b_ref, o_ref):
    h = jnp.dot(a_ref[...], w_ref[...], preferred_element_type=F32)
    o_ref[...] = _layer_norm_rows(DEEPNORM_ALPHA * res_ref[...] + h, g_ref[...], b_ref[...])


def _matmul_ln(a, w, res, g, b, tm, name):
    m, k = a.shape
    n = w.shape[1]
    return pl.pallas_call(
        _mm_ln_kernel,
        grid=(m // tm,),
        in_specs=[pl.BlockSpec((tm, k), lambda i: (i, 0)),
                  pl.BlockSpec((k, n), lambda i: (0, 0)),
                  pl.BlockSpec((tm, n), lambda i: (i, 0)),
                  pl.BlockSpec((1, n), lambda i: (0, 0)),
                  pl.BlockSpec((1, n), lambda i: (0, 0))],
        out_specs=pl.BlockSpec((tm, n), lambda i: (i, 0)),
        out_shape=jax.ShapeDtypeStruct((m, n), F32),
        compiler_params=_params("parallel"),
        name=name,
    )(a, w, res, g, b)


def _ffn_body(eid_ref, nused_ref, x_ref, wg_ref, wu_ref, wd_ref, acc_ref):
    j = pl.program_id(1)
    x = x_ref[...].astype(BF16)
    g = jnp.dot(x, wg_ref[0], preferred_element_type=F32)
    u = jnp.dot(x, wu_ref[0], preferred_element_type=F32)
    h = (_silu(g) * u).astype(BF16)
    part = jnp.dot(h, wd_ref[0], preferred_element_type=F32)

    @pl.when(j == 0)
    def _():
        acc_ref[...] = part

    @pl.when(j > 0)
    def _():
        acc_ref[...] += part


def _ffn_ln_kernel(eid_ref, nused_ref, x_ref, wg_ref, wu_ref, wd_ref, g_ref, b_ref,
                   o_ref, ob_ref, acc_ref):
    _ffn_body(eid_ref, nused_ref, x_ref, wg_ref, wu_ref, wd_ref, acc_ref)

    @pl.when(pl.program_id(1) == pl.num_programs(1) - 1)
    def _():
        o = _layer_norm_rows(DEEPNORM_ALPHA * x_ref[...] + acc_ref[...],
                             g_ref[...], b_ref[...])
        o_ref[...] = o
        ob_ref[...] = o.astype(BF16)


def _ffn_raw_kernel(eid_ref, nused_ref, x_ref, wg_ref, wu_ref, wd_ref, o_ref, acc_ref):
    i = pl.program_id(0)
    last = pl.program_id(1) == pl.num_programs(1) - 1
    used = i < nused_ref[0]

    @pl.when(used)
    def _():
        _ffn_body(eid_ref, nused_ref, x_ref, wg_ref, wu_ref, wd_ref, acc_ref)

    @pl.when(jnp.logical_and(used, last))
    def _():
        o_ref[...] = acc_ref[...]

    @pl.when(jnp.logical_and(jnp.logical_not(used), last))
    def _():
        o_ref[...] = jnp.zeros_like(o_ref)


def _ffn_specs(tm, d, tf, nf, fixed_weights=False):
    def wcol(i, j, eid, nused):
        return (eid[i], 0, jnp.where(i < nused[0], j, nf - 1))

    def wrow(i, j, eid, nused):
        return (eid[i], jnp.where(i < nused[0], j, nf - 1), 0)

    mode = dict(pipeline_mode=pl.Buffered(1)) if fixed_weights else {}
    return [pl.BlockSpec((tm, d), lambda i, j, eid, nused: (i, 0)),
            pl.BlockSpec((1, d, tf), wcol, **mode),
            pl.BlockSpec((1, d, tf), wcol, **mode),
            pl.BlockSpec((1, tf, d), wrow, **mode)]


def _ffn_ln(x, wg, wu, wd, g, b, tm, tf, name):
    m, d = x.shape
    f = wg.shape[2]
    nf = f // tf
    nblk = m // tm
    eid = jnp.zeros((nblk,), jnp.int32)
    nused = jnp.full((1,), nblk, jnp.int32)
    row = lambda i, j, eid, nused: (i, 0)
    vec = lambda i, j, eid, nused: (0, 0)
    return pl.pallas_call(
        _ffn_ln_kernel,
        grid_spec=pltpu.PrefetchScalarGridSpec(
            num_scalar_prefetch=2,
            grid=(nblk, nf),
            in_specs=_ffn_specs(tm, d, tf, nf, fixed_weights=(nf == 1)) + [
                pl.BlockSpec((1, d), vec),
                pl.BlockSpec((1, d), vec)],
            out_specs=[pl.BlockSpec((tm, d), row), pl.BlockSpec((tm, d), row)],
            scratch_shapes=[pltpu.VMEM((tm, d), F32)]),
        out_shape=[jax.ShapeDtypeStruct((m, d), F32),
                   jax.ShapeDtypeStruct((m, d), BF16)],
        compiler_params=_params("parallel", "arbitrary"),
        name=name,
    )(eid, nused, x, wg, wu, wd, g, b)


def _ffn_experts(x_buf, eid, nused, wg, wu, wd, tm, tf, name):
    m, d = x_buf.shape
    f = wg.shape[2]
    nf = f // tf
    nblk = m // tm
    row = lambda i, j, eid, nused: (i, 0)
    return pl.pallas_call(
        _ffn_raw_kernel,
        grid_spec=pltpu.PrefetchScalarGridSpec(
            num_scalar_prefetch=2,
            grid=(nblk, nf),
            in_specs=_ffn_specs(tm, d, tf, nf),
            out_specs=pl.BlockSpec((tm, d), row),
            scratch_shapes=[pltpu.VMEM((tm, d), F32)]),
        out_shape=jax.ShapeDtypeStruct((m, d), F32),
        compiler_params=_params("parallel", "arbitrary"),
        name=name,
    )(eid, nused, x_buf, wg, wu, wd)


def _split2(v):
    hi = v.astype(BF16)
    lo = (v - hi.astype(F32)).astype(BF16)
    return hi, lo


def _expand_cols(parts, e):
    out = jnp.dot(parts[0], e, preferred_element_type=F32)
    for p in parts[1:]:
        out += jnp.dot(p, e, preferred_element_type=F32)
    return out


def _ssd_kernel(xbc_ref, gate_ref, dt_ref, convw_ref, convb_ref, shift_ref, dtb_ref, alog_ref,
                dexp_ref, normw_ref, ltri_ref, sel128_ref, sel64_ref, y_ref,
                ext_ref, act_ref, eoff_ref, state_ref, *, d_inner, n_heads):
    L = SSM_CHUNK
    N = SSM_D_STATE
    G = SSM_GROUPS
    hpg = n_heads // G
    gw = d_inner // G
    hist = CONV_HISTORY_ROWS

    n_sub = xbc_ref.shape[0] // L

    @pl.when(pl.program_id(1) == 0)
    def _():
        ext_ref[0, 0:hist, :] = jnp.zeros((hist, ext_ref.shape[2]), BF16)
        state_ref[...] = jnp.zeros_like(state_ref)

    row = lax.broadcasted_iota(jnp.int32, (L, L), 0)
    col = lax.broadcasted_iota(jnp.int32, (L, L), 1)
    causal = col <= row
    glane = lax.broadcasted_iota(jnp.int32, (L, gw), 1) // SSM_HEAD_DIM

    for sc in range(n_sub):
        _ssd_chunk(sc, xbc_ref, gate_ref, dt_ref, convw_ref, convb_ref, shift_ref, dtb_ref,
                   alog_ref, dexp_ref, normw_ref, ltri_ref, sel128_ref, sel64_ref, y_ref,
                   ext_ref, act_ref.at[sc], eoff_ref.at[sc], state_ref, causal, glane,
                   d_inner=d_inner, hpg=hpg, gw=gw, n_sub=n_sub)


def _ssd_chunk(sc, xbc_ref, gate_ref, dt_ref, convw_ref, convb_ref, shift_ref, dtb_ref,
               alog_ref, dexp_ref, normw_ref, ltri_ref, sel128_ref, sel64_ref, y_ref,
               ext_ref, act_ref, eoff_ref, state_ref, causal, glane, *, d_inner, hpg, gw,
               n_sub):
    L = SSM_CHUNK
    N = SSM_D_STATE
    G = SSM_GROUPS
    hist = CONV_HISTORY_ROWS
    rows = slice(sc * L, (sc + 1) * L)

    dt = _softplus(dt_ref[rows, :] + dtb_ref[...])
    a = -jnp.exp(alog_ref[...])
    acum = jnp.dot(ltri_ref[...], dt * a, preferred_element_type=F32,
                   precision=lax.Precision.HIGHEST)
    acum_last = acum[L - 1:L, :]
    acum_t = acum.T
    dt_t = dt.T
    acum_parts = _split2(acum)

    xbc = xbc_ref[rows, :]
    if sc > 0:
        ext_ref[sc, 0:hist, :] = xbc_ref[sc * L - hist:sc * L, :]
    ext_ref[sc, hist:hist + L, :] = xbc
    shifted = jnp.dot(shift_ref[...], ext_ref[sc], preferred_element_type=F32)
    conv = convb_ref[...] + convw_ref[SSM_CONV - 1:SSM_CONV, :] * xbc.astype(F32)
    for k in range(SSM_CONV - 1):
        conv = conv + convw_ref[k:k + 1, :] * shifted[k * L:(k + 1) * L, :]
    if sc == n_sub - 1:
        ext_ref[0, 0:hist, :] = xbc_ref[n_sub * L - hist:n_sub * L, :]
    act_ref[...] = _silu(conv)

    eoff_ref[...] = _expand_cols(_split2(jnp.exp(acum)), sel64_ref[...])
    eend = _expand_cols(_split2(jnp.exp(acum_last - acum) * dt), sel64_ref[...])

    for g in range(G):
        xs_g = act_ref[:, g * gw:(g + 1) * gw]
        xs_b = xs_g.astype(BF16)
        b_g = act_ref[:, d_inner + g * N:d_inner + (g + 1) * N]
        c_gb = act_ref[:, d_inner + G * N + g * N:d_inner + G * N + (g + 1) * N].astype(BF16)
        b_gt = b_g.T.astype(BF16)
        cb = jnp.dot(c_gb, b_gt, preferred_element_type=F32)
        acol = _expand_cols(acum_parts,
                            sel128_ref[:, g * hpg * L:(g + 1) * hpg * L])
        ws = []
        xs_heads = []
        for h in range(hpg):
            hh = g * hpg + h
            seg = acol[:, h * L:(h + 1) * L] - acum_t[hh:hh + 1, :]
            decay = jnp.exp(jnp.where(causal, seg, -jnp.inf))
            ws.append((cb * decay * dt_t[hh:hh + 1, :]).astype(BF16))
            xs_heads.append(jnp.where(glane == h, xs_b, jnp.zeros_like(xs_b)))
        y_g = jnp.dot(jnp.concatenate(ws, axis=1), jnp.concatenate(xs_heads, axis=0),
                      preferred_element_type=F32)
        state_g = state_ref[:, g * gw:(g + 1) * gw]
        eoff_g = eoff_ref[:, g * gw:(g + 1) * gw]
        y_g = y_g + _bdot(c_gb, state_g) * eoff_g
        xs_scaled = xs_g * eend[:, g * gw:(g + 1) * gw]
        state_ref[:, g * gw:(g + 1) * gw] = (
            state_g * eoff_g[L - 1:L, :]
            + jnp.dot(b_gt, xs_scaled.astype(BF16), preferred_element_type=F32))
        y_g = y_g + dexp_ref[:, g * gw:(g + 1) * gw] * xs_g
        y_g = y_g * gate_ref[rows, g * gw:(g + 1) * gw].astype(F32)
        ms = jnp.mean(y_g * y_g, axis=-1, keepdims=True)
        y_g = y_g * lax.rsqrt(ms + RMS_EPS) * normw_ref[:, g * gw:(g + 1) * gw]
        y_ref[rows, g * gw:(g + 1) * gw] = y_g.astype(y_ref.dtype)


def _ssd(zx, dt_raw, conv_w, conv_b, dt_bias, a_log, d_skip, norm_w, batch, seq, d_inner,
         n_heads):
    L = SSM_CHUNK
    conv_dim = d_inner + 2 * SSM_GROUPS * SSM_D_STATE
    nc = seq // L
    pad = LANES_V7X - n_heads
    dtb = jnp.pad(dt_bias, (0, pad)).reshape(1, LANES_V7X)
    alog = jnp.pad(a_log, (0, pad)).reshape(1, LANES_V7X)
    dexp = jnp.repeat(d_skip, SSM_HEAD_DIM).reshape(1, d_inner)
    ltri = jnp.tril(jnp.ones((L, L), F32))
    hidx = jnp.arange(LANES_V7X)[:, None]
    sel128 = (hidx == (jnp.arange(n_heads * L) // L)[None, :]).astype(BF16)
    sel64 = (hidx == (jnp.arange(d_inner) // SSM_HEAD_DIM)[None, :]).astype(BF16)
    z_blk = conv_dim // d_inner
    t_idx = jnp.arange((SSM_CONV - 1) * L)
    src = t_idx % L + CONV_HISTORY_ROWS - (SSM_CONV - 1) + t_idx // L
    shift = (src[:, None] == jnp.arange(L + CONV_HISTORY_ROWS)[None, :]).astype(BF16)
    const = lambda b, c: (0, 0)
    kern = functools.partial(_ssd_kernel, d_inner=d_inner, n_heads=n_heads)
    n_sub = SSD_CHUNKS_PER_STEP
    rows = n_sub * L
    nc = nc // n_sub
    return pl.pallas_call(
        kern,
        grid=(batch, nc),
        in_specs=[pl.BlockSpec((rows, conv_dim), lambda b, c: (b * nc + c, 0)),
                  pl.BlockSpec((rows, d_inner), lambda b, c: (b * nc + c, z_blk)),
                  pl.BlockSpec((rows, LANES_V7X), lambda b, c: (b * nc + c, 0)),
                  pl.BlockSpec((SSM_CONV, conv_dim), const),
                  pl.BlockSpec((1, conv_dim), const),
                  pl.BlockSpec(((SSM_CONV - 1) * L, L + CONV_HISTORY_ROWS), const),
                  pl.BlockSpec((1, LANES_V7X), const),
                  pl.BlockSpec((1, LANES_V7X), const),
                  pl.BlockSpec((1, d_inner), const),
                  pl.BlockSpec((1, d_inner), const),
                  pl.BlockSpec((L, L), const),
                  pl.BlockSpec((LANES_V7X, n_heads * L), const),
                  pl.BlockSpec((LANES_V7X, d_inner), const)],
        out_specs=pl.BlockSpec((rows, d_inner), lambda b, c: (b * nc + c, 0)),
        out_shape=jax.ShapeDtypeStruct((batch * seq, d_inner), BF16),
        scratch_shapes=[pltpu.VMEM((n_sub, L + CONV_HISTORY_ROWS, conv_dim), BF16),
                        pltpu.VMEM((n_sub, L, conv_dim), F32),
                        pltpu.VMEM((n_sub, L, d_inner), F32),
                        pltpu.VMEM((SSM_D_STATE, d_inner), F32)],
        compiler_params=_params("parallel", "arbitrary"),
        name="ssd_scan",
    )(zx, zx, dt_raw, conv_w, conv_b.reshape(1, conv_dim), shift, dtb, alog, dexp,
      norm_w.reshape(1, d_inner), ltri, sel128, sel64)


def _attn_kernel(q_ref, k_ref, v_ref, mbd_ref, o_ref, kk_ref, vv_ref, acc_ref, kn_ref, z_ref,
                 *, tq):
    tk = ATTN_K_TILE
    qi = pl.program_id(2)
    n_kb = k_ref.shape[0] // tk
    n_diag = tq // tk
    lane = lax.broadcasted_iota(jnp.int32, (tk, LANES_V7X), 1)
    head0 = lane < SB_HEAD_DIM

    n_pairs = q_ref.shape[1] // LANES_V7X
    rows = SUBLANES_V7X

    @pl.when(qi == 0)
    def _():
        for c in range(n_pairs):
            cols = slice(c * LANES_V7X, (c + 1) * LANES_V7X)

            def fill(j, carry, c=c, cols=cols):
                m0, m1 = carry
                kb = k_ref[pl.ds(pl.multiple_of(j * tk, tk), tk), cols]
                vb = v_ref[pl.ds(pl.multiple_of(j * tk, tk), tk), cols]
                zero = jnp.zeros_like(kb)
                kk_ref[c, j, 0:tk, :] = jnp.where(head0, kb, zero)
                kk_ref[c, j, tk:2 * tk, :] = jnp.where(head0, zero, kb)
                vv_ref[c, j, 0:tk, :] = jnp.where(head0, vb, zero)
                vv_ref[c, j, tk:2 * tk, :] = jnp.where(head0, zero, vb)
                ksq = kb.astype(F32) * kb.astype(F32)
                n0 = jnp.sum(jnp.where(head0, ksq, 0.0), axis=-1, keepdims=True)
                n1 = jnp.sum(jnp.where(head0, 0.0, ksq), axis=-1, keepdims=True)
                return jnp.maximum(m0, n0), jnp.maximum(m1, n1)
            init = (jnp.zeros((tk, 1), F32), jnp.zeros((tk, 1), F32))
            m0, m1 = lax.fori_loop(0, n_kb, fill, init, unroll=4)
            kn_ref[(2 * c) * rows:(2 * c + 1) * rows, :] = jnp.broadcast_to(
                jnp.max(m0, axis=0, keepdims=True), (rows, LANES_V7X))
            kn_ref[(2 * c + 1) * rows:(2 * c + 2) * rows, :] = jnp.broadcast_to(
                jnp.max(m1, axis=0, keepdims=True), (rows, LANES_V7X))

    acc_ref[...] = jnp.zeros_like(acc_ref)
    sub = min(ATTN_SUB_ROWS, tq)
    n_sub = tq // sub
    s_off = lax.broadcasted_iota(jnp.int32, (sub, 2 * tk), 1) % tk
    row = lax.broadcasted_iota(jnp.int32, (sub, 2 * tk), 0)
    qhead0 = lax.broadcasted_iota(jnp.int32, (tq, LANES_V7X), 1) < SB_HEAD_DIM

    chains = [(c, s) for c in range(n_pairs) for s in range(n_sub)]
    qs = {}
    zbmax = []
    for c in range(n_pairs):
        q = q_ref[:, c * LANES_V7X:(c + 1) * LANES_V7X]
        qsq = q.astype(F32) * q.astype(F32)
        zb_pair = []
        for h, sel in enumerate((qhead0, jnp.logical_not(qhead0))):
            qn = jnp.max(jnp.sum(jnp.where(sel, qsq, 0.0), axis=-1, keepdims=True),
                         axis=0, keepdims=True)
            kn = kn_ref[(2 * c + h) * rows:(2 * c + h) * rows + 1, 0:1]
            zb_pair.append(jnp.sqrt(qn * kn) * SCORE_BOUND_SLACK + SCORE_BOUND_SLACK)
        for s in range(n_sub):
            qs[(c, s)] = q[s * sub:(s + 1) * sub, :]
            zbmax.extend(zb_pair)

    def scores(n, j):
        return lax.dot_general(qs[chains[n]], kk_ref[chains[n][0], j], (((1,), (1,)), ((), ())),
                               preferred_element_type=F32)

    def run(jobs, accs, zs=None, nxt=None):
        old = list(accs)
        new = list(accs)
        if zs is None:
            zs = [scores(n, j) for n, j, _ in jobs]
        sps = []
        for z, (n, _, mask) in zip(zs, jobs):
            sp = jnp.maximum(z, 0.0) + jnp.log2(1.0 + jnp.exp2(-jnp.abs(z)))
            sp = sp if mask is None else jnp.where(mask, sp, 0.0)
            sps.append(sp)
            new[2 * n] = old[2 * n] + jnp.sum(sp[:, :tk], axis=-1, keepdims=True)
            new[2 * n + 1] = old[2 * n + 1] + jnp.sum(sp[:, tk:], axis=-1, keepdims=True)
        incls = [jnp.dot(sp.astype(BF16), mbd_ref[...], preferred_element_type=F32)
                 for sp in sps]
        if nxt is not None:
            for n, _, _ in jobs:
                z_ref[n] = scores(n, nxt)
        ps = []
        for z, incl, (n, _, mask) in zip(zs, incls, jobs):
            la0 = z[:, :tk] - incl[:, :tk] - old[2 * n]
            la1 = z[:, tk:] - incl[:, tk:] - old[2 * n + 1]
            p = jnp.concatenate([jnp.exp2(la0), jnp.exp2(la1)], axis=1)
            ps.append((p if mask is None else jnp.where(mask, p, 0.0)).astype(BF16))
        for p, (n, j, _) in zip(ps, jobs):
            c, s = chains[n]
            acc_ref[s * sub:(s + 1) * sub, c * LANES_V7X:(c + 1) * LANES_V7X] += jnp.dot(
                p, vv_ref[c, j], preferred_element_type=F32)
        return new

    def live(accs):
        worst = zbmax[0] - jnp.min(accs[0], axis=0, keepdims=True)
        for zb, a in zip(zbmax[1:], accs[1:]):
            worst = jnp.maximum(worst, zb - jnp.min(a, axis=0, keepdims=True))
        return (jnp.max(worst) > F32_MIN_EXP2).astype(jnp.int32)

    accs = [jnp.zeros((sub, 1), F32) for _ in range(2 * len(chains))]
    j_diag = qi * n_diag
    for d in range(n_diag - 1, -1, -1):
        jobs = []
        for n, (c, s) in enumerate(chains):
            if d * tk >= (s + 1) * sub - 1:
                continue
            all_visible = (d + 1) * tk <= s * sub
            jobs.append((n, j_diag + d, None if all_visible else d * tk + s_off < s * sub + row))
        accs = run(jobs, accs, nxt=jnp.maximum(j_diag - 1, 0) if d == 0 else None)

    def cond(carry):
        return jnp.logical_and(carry[0] < j_diag, carry[1] > 0)

    def body(carry):
        j = j_diag - 1 - carry[0]
        zs = [z_ref[n] for n in range(len(chains))]
        accs = run([(n, j, None) for n in range(len(chains))], carry[2:], zs=zs,
                   nxt=jnp.maximum(j - 1, 0))
        return (carry[0] + 1, live(accs)) + tuple(accs)

    lax.while_loop(cond, body, (jnp.int32(0), live(accs)) + tuple(accs))
    o_ref[...] = acc_ref[...].astype(o_ref.dtype)


def _attention(qkv, batch, seq, d_model):
    tq = min(ATTN_Q_TILE, seq)
    tk = ATTN_K_TILE
    nq = seq // tq
    width = ATTN_PAIRS_PER_STEP * LANES_V7X
    n_groups = d_model // width
    r = jnp.arange(2 * tk)
    mbd = ((r[:, None] // tk == r[None, :] // tk) & (r[:, None] >= r[None, :])).astype(BF16)
    kern = functools.partial(_attn_kernel, tq=tq)
    return pl.pallas_call(
        kern,
        grid=(batch, n_groups, nq),
        in_specs=[pl.BlockSpec((tq, width), lambda b, p, i: (b * nq + i, p)),
                  pl.BlockSpec((seq, width), lambda b, p, i: (b, n_groups + p)),
                  pl.BlockSpec((seq, width), lambda b, p, i: (b, 2 * n_groups + p)),
                  pl.BlockSpec((2 * tk, 2 * tk), lambda b, p, i: (0, 0))],
        out_specs=pl.BlockSpec((tq, width), lambda b, p, i: (b * nq + i, p)),
        out_shape=jax.ShapeDtypeStruct((batch * seq, d_model), BF16),
        scratch_shapes=[pltpu.VMEM((ATTN_PAIRS_PER_STEP, seq // tk, 2 * tk, LANES_V7X), BF16),
                        pltpu.VMEM((ATTN_PAIRS_PER_STEP, seq // tk, 2 * tk, LANES_V7X), BF16),
                        pltpu.VMEM((tq, width), F32),
                        pltpu.VMEM((2 * ATTN_PAIRS_PER_STEP * SUBLANES_V7X, LANES_V7X), F32),
                        pltpu.VMEM((ATTN_PAIRS_PER_STEP * (tq // min(ATTN_SUB_ROWS, tq)),
                                    min(ATTN_SUB_ROWS, tq), 2 * tk), F32)],
        compiler_params=_params("parallel", "parallel", "arbitrary"),
        name="stickbreak_attn",
    )(qkv, qkv, qkv, mbd)


def _router_kernel(x_ref, w_ref, b_ref, ltri_ref, info_ref, cnt_ref, carry_ref):
    tm = x_ref.shape[0]

    @pl.when(pl.program_id(0) == 0)
    def _():
        carry_ref[...] = jnp.zeros_like(carry_ref)

    xh, xl = _split2(x_ref[...])
    logits = (jnp.dot(xh, w_ref[0], preferred_element_type=F32)
              + jnp.dot(xl, w_ref[0], preferred_element_type=F32)
              + jnp.dot(xh, w_ref[1], preferred_element_type=F32)) + b_ref[...]
    lane = lax.broadcasted_iota(jnp.int32, (tm, LANES_V7X), 1).astype(F32)
    big = float(LANES_V7X)
    m1 = jnp.max(logits, axis=-1, keepdims=True)
    i1 = jnp.min(jnp.where(logits == m1, lane, big), axis=-1, keepdims=True)
    l2 = jnp.where(lane == i1, -jnp.inf, logits)
    m2 = jnp.max(l2, axis=-1, keepdims=True)
    i2 = jnp.min(jnp.where(l2 == m2, lane, big), axis=-1, keepdims=True)
    e = jnp.exp(m2 - m1)
    g1 = 1.0 / (1.0 + e)
    g2 = e * g1
    hit1 = lane == i1
    hit2 = lane == i2
    oh = jnp.where(jnp.logical_or(hit1, hit2), 1.0, 0.0)
    cum = jnp.dot(ltri_ref[...], oh.astype(BF16), preferred_element_type=F32)
    before = cum - oh + carry_ref[0:1, :]
    r1 = jnp.sum(jnp.where(hit1, before, 0.0), axis=-1, keepdims=True)
    r2 = jnp.sum(jnp.where(hit2, before, 0.0), axis=-1, keepdims=True)
    total = carry_ref[0:1, :] + cum[tm - 1:tm, :]
    carry_ref[...] = jnp.broadcast_to(total, carry_ref.shape)
    cnt_ref[...] = jnp.broadcast_to(total, cnt_ref.shape)
    info = jnp.where(lane == 0.0, i1, 0.0)
    info = jnp.where(lane == 1.0, i2, info)
    info = jnp.where(lane == 2.0, g1, info)
    info = jnp.where(lane == 3.0, g2, info)
    info = jnp.where(lane == 4.0, r1, info)
    info = jnp.where(lane == 5.0, r2, info)
    info_ref[...] = info


def _router(x, w_router, b_router):
    t, d = x.shape
    tm = min(ROUTE_TILE, t)
    pad = LANES_V7X - N_EXPERTS
    w = jnp.stack(_split2(jnp.pad(w_router, ((0, 0), (0, pad)))))
    b = jnp.concatenate([b_router.astype(F32), jnp.full((pad,), -1e30, F32)]).reshape(1, LANES_V7X)
    ltri = jnp.tril(jnp.ones((tm, tm), BF16))
    return pl.pallas_call(
        _router_kernel,
        grid=(t // tm,),
        in_specs=[pl.BlockSpec((tm, d), lambda i: (i, 0)),
                  pl.BlockSpec((2, d, LANES_V7X), lambda i: (0, 0, 0)),
                  pl.BlockSpec((1, LANES_V7X), lambda i: (0, 0)),
                  pl.BlockSpec((tm, tm), lambda i: (0, 0))],
        out_specs=[pl.BlockSpec((tm, LANES_V7X), lambda i: (i, 0)),
                   pl.BlockSpec((SUBLANES_V7X, LANES_V7X), lambda i: (0, 0))],
        out_shape=[jax.ShapeDtypeStruct((t, LANES_V7X), F32),
                   jax.ShapeDtypeStruct((SUBLANES_V7X, LANES_V7X), F32)],
        scratch_shapes=[pltpu.VMEM((SUBLANES_V7X, LANES_V7X), F32)],
        compiler_params=_params("arbitrary"),
        name="moe_router",
    )(x, w, b, ltri)


def _row_copy(src_ref, src_row, dst_ref, dst_row, sem):
    return pltpu.make_async_copy(src_ref.at[pl.ds(src_row, 1)],
                                 dst_ref.at[pl.ds(dst_row, 1)], sem)


def _dispatch_kernel(d1_ref, d2_ref, x_ref, buf_in_hbm, buf_hbm, sem):
    del buf_in_hbm
    tm = d1_ref.shape[0]

    def issue(t, c):
        _row_copy(x_ref, t, buf_hbm, d1_ref[t], sem).start()
        _row_copy(x_ref, t, buf_hbm, d2_ref[t], sem).start()
        return c
    lax.fori_loop(0, tm, issue, 0, unroll=8)

    def drain(t, c):
        _row_copy(x_ref, 0, buf_hbm, 0, sem).wait()
        _row_copy(x_ref, 0, buf_hbm, 0, sem).wait()
        return c
    lax.fori_loop(0, tm, drain, 0, unroll=8)


def _dispatch(x, dest1, dest2, n_slots):
    t, d = x.shape
    tm = min(ROUTE_TILE, t)
    buf0 = jnp.zeros((n_slots, d), x.dtype)
    smem = lambda: pl.BlockSpec((tm,), lambda i: (i,), memory_space=pltpu.SMEM)
    return pl.pallas_call(
        _dispatch_kernel,
        grid=(t // tm,),
        in_specs=[smem(), smem(),
                  pl.BlockSpec((tm, d), lambda i: (i, 0)),
                  pl.BlockSpec(memory_space=pl.ANY)],
        out_specs=pl.BlockSpec(memory_space=pl.ANY),
        out_shape=jax.ShapeDtypeStruct((n_slots, d), x.dtype),
        scratch_shapes=[pltpu.SemaphoreType.DMA],
        input_output_aliases={3: 0},
        compiler_params=_params("arbitrary"),
        name="moe_dispatch",
    )(dest1, dest2, x, buf0)


def _combine_kernel(d1_ref, d2_ref, info_ref, y_hbm, res_ref, g_ref, b_ref, o_ref,
                    buf1_ref, buf2_ref, sem):
    tm = d1_ref.shape[0]

    def issue(t, c):
        _row_copy(y_hbm, d1_ref[t], buf1_ref, t, sem).start()
        _row_copy(y_hbm, d2_ref[t], buf2_ref, t, sem).start()
        return c
    lax.fori_loop(0, tm, issue, 0, unroll=8)

    def drain(t, c):
        _row_copy(y_hbm, 0, buf1_ref, 0, sem).wait()
        _row_copy(y_hbm, 0, buf2_ref, 0, sem).wait()
        return c
    lax.fori_loop(0, tm, drain, 0, unroll=8)

    g1 = info_ref[:, 2:3]
    g2 = info_ref[:, 3:4]
    y = g1 * buf1_ref[...] + g2 * buf2_ref[...]
    o_ref[...] = _layer_norm_rows(DEEPNORM_ALPHA * res_ref[...] + y, g_ref[...], b_ref[...])


def _combine_ln(y_buf, dest1, dest2, info, res, g, b):
    t, d = res.shape
    tm = min(ROUTE_TILE, t)
    smem = lambda: pl.BlockSpec((tm,), lambda i: (i,), memory_space=pltpu.SMEM)
    return pl.pallas_call(
        _combine_kernel,
        grid=(t // tm,),
        in_specs=[smem(), smem(),
                  pl.BlockSpec((tm, LANES_V7X), lambda i: (i, 0)),
                  pl.BlockSpec(memory_space=pl.ANY),
                  pl.BlockSpec((tm, d), lambda i: (i, 0)),
                  pl.BlockSpec((1, d), lambda i: (0, 0)),
                  pl.BlockSpec((1, d), lambda i: (0, 0))],
        out_specs=pl.BlockSpec((tm, d), lambda i: (i, 0)),
        out_shape=jax.ShapeDtypeStruct((t, d), F32),
        scratch_shapes=[pltpu.VMEM((tm, d), F32), pltpu.VMEM((tm, d), F32),
                        pltpu.SemaphoreType.DMA],
        compiler_params=_params("arbitrary"),
        name="moe_combine_ln",
    )(dest1, dest2, info, y_buf, res, g, b)


def _pad_cols(w, n):
    return jnp.pad(w, ((0, 0), (0, n - w.shape[1])))


def _mamba_layer(x, xb, batch, seq, ln_g, ln_b, w_in, conv_w, conv_b, dt_bias, a_log, d_skip,
                 norm_w, w_out, w_gate, w_up, w_down):
    d_model = x.shape[1]
    n_heads = a_log.shape[0]
    d_inner = n_heads * SSM_HEAD_DIM
    conv_dim = conv_w.shape[1]
    w_z, w_xbc, w_dt = jnp.split(w_in, [d_inner, d_inner + conv_dim], axis=1)
    w_main = jnp.concatenate([w_xbc, w_z], axis=1).astype(BF16)
    zx = _matmul_resident(xb, w_main, BF16, min(ROW_TILE, x.shape[0]), 2 * ROW_TILE,
                          "ssm_in_proj", silu_from=conv_dim)
    dt_raw = _matmul(xb, _pad_cols(w_dt, LANES_V7X).astype(BF16), F32,
                     min(ROW_TILE * 2, x.shape[0]), LANES_V7X, "ssm_dt_proj")
    y = _ssd(zx, dt_raw, conv_w, conv_b, dt_bias, a_log, d_skip, norm_w, batch, seq, d_inner,
             n_heads)
    g0 = ln_g[0].reshape(1, d_model)
    b0 = ln_b[0].reshape(1, d_model)
    x1 = _matmul_ln(y, w_out.astype(BF16), x, g0, b0, min(ROW_TILE, x.shape[0]),
                         "ssm_out_proj_ln")
    f = w_gate.shape[1]
    fp = -(-f // (2 * LANES_V7X)) * (2 * LANES_V7X)
    wg = _pad_cols(w_gate, fp).astype(BF16)[None]
    wu = _pad_cols(w_up, fp).astype(BF16)[None]
    wd = jnp.pad(w_down, ((0, fp - f), (0, 0))).astype(BF16)[None]
    g1 = ln_g[1].reshape(1, d_model)
    b1 = ln_b[1].reshape(1, d_model)
    return _ffn_ln(x1, wg, wu, wd, g1, b1, min(ROW_TILE, x.shape[0]), fp,
                   "dense_swiglu_ln")


def _attn_moe_layer(x, xb, batch, seq, ln_g, ln_b, w_qkv, w_o, w_router, b_router,
                    w_gate, w_up, w_down):
    t, d_model = x.shape
    qscale = jnp.concatenate([jnp.full((d_model,), LOG2E / math.sqrt(SB_HEAD_DIM), F32),
                              jnp.ones((2 * d_model,), F32)])
    w_qkv_b = (w_qkv * qscale).astype(BF16)
    qkv = _matmul_resident(xb, w_qkv_b, BF16, min(ROW_TILE * 2, t), d_model, "sb_qkv_proj")
    o = _attention(qkv, batch, seq, d_model)
    g0 = ln_g[0].reshape(1, d_model)
    b0 = ln_b[0].reshape(1, d_model)
    x1 = _matmul_ln(o, w_o.astype(BF16), x, g0, b0, min(ROW_TILE, t), "sb_out_proj_ln")

    info, cnt = _router(x1, w_router, b_router)
    counts = cnt[0, :N_EXPERTS].astype(jnp.int32)
    padded = (counts + MOE_ROWS - 1) // MOE_ROWS * MOE_ROWS
    pad_end = jnp.cumsum(padded)
    pad_start = pad_end - padded
    e1 = info[:, 0].astype(jnp.int32)
    e2 = info[:, 1].astype(jnp.int32)
    dest1 = pad_start[e1] + info[:, 4].astype(jnp.int32)
    dest2 = pad_start[e2] + info[:, 5].astype(jnp.int32)
    n_slots = 2 * t + N_EXPERTS * MOE_ROWS
    n_blocks = n_slots // MOE_ROWS
    block_start = jnp.arange(n_blocks) * MOE_ROWS
    block_expert = jnp.minimum(
        jnp.sum(pad_end[None, :] <= block_start[:, None], axis=1), N_EXPERTS - 1).astype(jnp.int32)
    n_used = (pad_end[-1] // MOE_ROWS).astype(jnp.int32).reshape(1)

    x_buf = _dispatch(x1, dest1, dest2, n_slots)
    f = w_gate.shape[2]
    y_buf = _ffn_experts(x_buf, block_expert, n_used, w_gate.astype(BF16), w_up.astype(BF16),
                         w_down.astype(BF16), MOE_ROWS, f // 2, "moe_expert_swiglu")
    g1 = ln_g[1].reshape(1, d_model)
    b1 = ln_b[1].reshape(1, d_model)
    return _combine_ln(y_buf, dest1, dest2, info, x1, g1, b1)


def kernel(x, ln_g, ln_b, ssm_w_in, ssm_conv_w, ssm_conv_b, ssm_dt_bias, ssm_a_log, ssm_d,
           ssm_norm_w, ssm_w_out, sb_w_qkv, sb_w_o, ffn_w_gate, ffn_w_up, ffn_w_down,
           moe_w_router, moe_b_router, moe_w_gate, moe_w_up, moe_w_down):
    batch, seq, d_model = x.shape
    xf = x.reshape(batch * seq, d_model)
    x1, x1b = _mamba_layer(xf, xf.astype(BF16), batch, seq, ln_g[0], ln_b[0], ssm_w_in[0],
                           ssm_conv_w[0], ssm_conv_b[0], ssm_dt_bias[0], ssm_a_log[0], ssm_d[0],
                           ssm_norm_w[0], ssm_w_out[0], ffn_w_gate[0], ffn_w_up[0], ffn_w_down[0])
    out = _attn_moe_layer(x1, x1b, batch, seq, ln_g[1], ln_b[1], sb_w_qkv[0], sb_w_o[0],
                          moe_w_router[0], moe_b_router[0], moe_w_gate[0], moe_w_up[0],
                          moe_w_down[0])
    return out.reshape(batch, seq, d_model)
```
